```python
import jax, jax.numpy as jnp
from jax import lax
import numpy as np

D_MODEL = 1024
BATCH = 8
SEQ = 2048
DEPTH = 2

N_META = 16
EPS = 1e-6
ROPE_THETA = 10000.0
Q_BLOCK = 128
N_BRANCHES = 3

POOL_WINDOWS = (2, 4, 8, 16)
N_POOL_GROUPS = 4
C_A = D_MODEL
POOL_GROUP_DIM = C_A // N_POOL_GROUPS

RWKV_HEAD_DIM = 64
C_B = D_MODEL
RWKV_HEADS = C_B // RWKV_HEAD_DIM
RWKV_DECAY_RANK = 64
RWKV_A_RANK = 64
RWKV_GATE_RANK = 128
RWKV_VRES_RANK = 32
RWKV_GN_EPS = 64e-5

DSA_HEADS = 16
DSA_NOPE_DIM = 64
DSA_ROPE_DIM = 32
DSA_V_DIM = 64
C_C = DSA_HEADS * DSA_V_DIM
DSA_Q_RANK = 256
DSA_KV_RANK = 128
IDX_HEADS = 8
IDX_DIM = 64
IDX_ROPE_DIM = 32
MAX_TOPK = 256

N_GROUPS = 4
EXPERTS_PER_GROUP = 4
N_EXPERTS = N_GROUPS * EXPERTS_PER_GROUP
EXPERT_TOPK = 2
EXPERT_HIDDEN = 256

C_SHIFT = 3 * C_B + RWKV_DECAY_RANK + RWKV_A_RANK + RWKV_GATE_RANK
C_DSA_IN = DSA_Q_RANK + DSA_KV_RANK + DSA_ROPE_DIM + IDX_DIM + IDX_HEADS
C_IN = N_BRANCHES * D_MODEL + C_A + C_SHIFT + C_DSA_IN

kernel_name = 'hybrid_pool_rwkv7_dsa_hiermoe'


def rms_norm(x, g, eps=EPS):
    xf = x.astype(jnp.float32)
    y = xf * lax.rsqrt(jnp.mean(xf * xf, axis=-1, keepdims=True) + eps)
    return (y * g.astype(jnp.float32)).astype(x.dtype)


def layer_norm(x, w, b, eps=EPS):
    xf = x.astype(jnp.float32)
    mu = jnp.mean(xf, axis=-1, keepdims=True)
    var = jnp.mean(jnp.square(xf - mu), axis=-1, keepdims=True)
    y = (xf - mu) * lax.rsqrt(var + eps)
    return (y * w.astype(jnp.float32) + b.astype(jnp.float32)).astype(x.dtype)


def rope(x, pos):
    d = x.shape[-1]
    inv = ROPE_THETA ** (-jnp.arange(0, d, 2, dtype=jnp.float32) / d)
    ang = pos.astype(jnp.float32)[:, None] * inv[None, :]
    ang = ang.reshape(ang.shape[:1] + (1,) * (x.ndim - 3) + ang.shape[1:])
    cos, sin = jnp.cos(ang), jnp.sin(ang)
    xf = x.astype(jnp.float32)
    x1, x2 = xf[..., : d // 2], xf[..., d // 2:]
    return jnp.concatenate([x1 * cos - x2 * sin, x2 * cos + x1 * sin], axis=-1).astype(x.dtype)


def token_shift(p, mu):
    prev = jnp.pad(p[:, :-1], ((0, 0), (1, 0), (0, 0)))
    return p + (prev - p) * mu.astype(p.dtype)


def pool_mixer(u, w_grp, scale):
    B, T, _ = u.shape
    uf = u.astype(jnp.float32).reshape(B, T, N_POOL_GROUPS, POOL_GROUP_DIM)
    cs = jnp.cumsum(uf, axis=1)
    t = jnp.arange(T, dtype=jnp.float32)
    means = []
    for g, w in enumerate(POOL_WINDOWS):
        c = cs[:, :, g]
        lo = jnp.pad(c, ((0, 0), (w, 0), (0, 0)))[:, :T]
        cnt = jnp.minimum(t + 1.0, float(w))[None, :, None]
        means.append((c - lo) / cnt)
    pooled = jnp.stack(means, axis=2) - uf
    y = jnp.einsum('btgc,gcd->btgd', pooled.astype(u.dtype), w_grp).reshape(B, T, C_A)
    return y * scale


def rwkv7_scan(r, w, k, v, a, b):
    def step(S, inp):
        r_t, w_t, k_t, v_t, a_t, b_t = inp
        sa = jnp.einsum('bhij,bhj->bhi', S, a_t)
        S = S * w_t[:, :, None, :] + sa[..., :, None] * b_t[..., None, :] + v_t[..., :, None] * k_t[..., None, :]
        return S, jnp.einsum('bhij,bhj->bhi', S, r_t)
    B, T, H, N = r.shape
    S0 = jnp.zeros((B, H, N, N), jnp.float32)
    xs = tuple(jnp.moveaxis(z, 1, 0) for z in (r, w, k, v, a, b))
    _, out = lax.scan(step, S0, xs)
    return jnp.moveaxis(out, 0, 1)


def rwkv7_mixer(pb, vres, v_first, w0, w2, a0, a2, g2, k_k, k_a, r_k, ln_w, ln_b):
    B, T, _ = pb.shape
    H, N = RWKV_HEADS, RWKV_HEAD_DIM
    f32 = jnp.float32
    cuts = [C_B, 2 * C_B, 3 * C_B, 3 * C_B + RWKV_DECAY_RANK, 3 * C_B + RWKV_DECAY_RANK + RWKV_A_RANK]
    r, k, v, wd, ad, gd = jnp.split(pb, cuts, axis=-1)
    w_log = -jax.nn.softplus(-(w0 + jnp.tanh(wd) @ w2).astype(f32)) - 0.5
    decay = jnp.exp(-jnp.exp(w_log))
    a = jax.nn.sigmoid((a0 + ad @ a2).astype(f32))
    g = jax.nn.sigmoid(gd) @ g2
    if vres is None:
        v_first = v
    else:
        vd, v0, v2 = vres
        v = v + (v_first - v) * jax.nn.sigmoid(v0 + vd @ v2)
    heads = lambda z: z.astype(f32).reshape(B, T, H, N)
    rh, vh, ah, dh = heads(r), heads(v), heads(a), heads(decay)
    kk = heads(k * k_k)
    kk = kk / jnp.maximum(jnp.linalg.norm(kk, axis=-1, keepdims=True), 1e-12)
    kh = heads(k) * (1.0 + (ah - 1.0) * k_a.astype(f32).reshape(H, N))
    o = rwkv7_scan(rh, dh, kh, vh, -kk, kk * ah)
    mu = jnp.mean(o, axis=-1, keepdims=True)
    var = jnp.mean(jnp.square(o - mu), axis=-1, keepdims=True)
    o = ((o - mu) * lax.rsqrt(var + RWKV_GN_EPS)).reshape(B, T, C_B)
    o = o * ln_w.astype(f32) + ln_b.astype(f32)
    bonus = jnp.sum(rh * kh * r_k.astype(f32), axis=-1, keepdims=True) * vh
    o = (o + bonus.reshape(B, T, C_B)) * g.astype(f32)
    return o.astype(pb.dtype), v_first


def dsa_mixer(pc, pos, topk, q_norm_g, kv_norm_g, w_uq, w_uk, w_uv, w_q_idx, k_idx_w, k_idx_b):
    B, T, _ = pc.shape
    f32 = jnp.float32
    cuts = [DSA_Q_RANK, DSA_Q_RANK + DSA_KV_RANK, DSA_Q_RANK + DSA_KV_RANK + DSA_ROPE_DIM,
            DSA_Q_RANK + DSA_KV_RANK + DSA_ROPE_DIM + IDX_DIM]
    cq, ckv, krope, kidx, widx = jnp.split(pc, cuts, axis=-1)
    cq = rms_norm(cq, q_norm_g)
    ckv = rms_norm(ckv, kv_norm_g)
    q = (cq @ w_uq).reshape(B, T, DSA_HEADS, DSA_NOPE_DIM + DSA_ROPE_DIM)
    q_nope, q_rope = q[..., :DSA_NOPE_DIM], rope(q[..., DSA_NOPE_DIM:], pos)
    k_rope = rope(krope, pos)
    q_lat = jnp.einsum('bthd,hcd->bthc', q_nope, w_uk)
    q_idx = (cq @ w_q_idx).reshape(B, T, IDX_HEADS, IDX_DIM)
    q_idx = jnp.concatenate([rope(q_idx[..., :IDX_ROPE_DIM], pos), q_idx[..., IDX_ROPE_DIM:]], axis=-1)
    k_idx = layer_norm(kidx, k_idx_w, k_idx_b)
    k_idx = jnp.concatenate([rope(k_idx[..., :IDX_ROPE_DIM], pos), k_idx[..., IDX_ROPE_DIM:]], axis=-1)
    w_head = widx * (IDX_HEADS ** -0.5 * IDX_DIM ** -0.5)
    scale = (DSA_NOPE_DIM + DSA_ROPE_DIM) ** -0.5
    n_blk = -(-T // Q_BLOCK)
    t_pad = n_blk * Q_BLOCK

    def blocks(z):
        z = jnp.pad(z, ((0, 0), (0, t_pad - T)) + ((0, 0),) * (z.ndim - 2))
        return jnp.moveaxis(z.reshape((B, n_blk, Q_BLOCK) + z.shape[2:]), 1, 0)

    qpos = jnp.arange(t_pad, dtype=jnp.int32).reshape(n_blk, Q_BLOCK)
    key_pos = jnp.arange(T, dtype=jnp.int32)
    gather = jax.vmap(lambda tbl, ix: tbl[ix])

    def attend(args):
        ql, qr, qi, wh, tq = args
        score_idx = jnp.einsum('bqhs,bqh->bqs', jax.nn.relu(jnp.einsum('bqhd,bsd->bqhs', qi, k_idx)), wh)
        admissible = key_pos[None, None, :] <= tq[None, :, None]
        score_idx = jnp.where(admissible, score_idx.astype(f32), -jnp.inf)
        _, sel = lax.top_k(score_idx, topk)
        ckv_sel = gather(ckv, sel)
        kr_sel = gather(k_rope, sel)
        s = jnp.einsum('bqhc,bqkc->bqhk', ql, ckv_sel) + jnp.einsum('bqhr,bqkr->bqhk', qr, kr_sel)
        s = s.astype(f32) * scale
        valid = (sel <= tq[None, :, None])[:, :, None, :]
        p = jax.nn.softmax(jnp.where(valid, s, -jnp.inf), axis=-1).astype(ckv_sel.dtype)
        return jnp.einsum('bqhk,bqkc->bqhc', p, ckv_sel)

    o_lat = lax.map(attend, (blocks(q_lat), blocks(q_rope), blocks(q_idx), blocks(w_head), qpos))
    o_lat = jnp.moveaxis(o_lat, 0, 1).reshape(B, t_pad, DSA_HEADS, DSA_KV_RANK)[:, :T]
    return jnp.einsum('bthc,hcv->bthv', o_lat, w_uv).reshape(B, T, C_C)


def hier_moe(h, w_grp, b_grp, w_exp, b_exp, w1, w3, w2):
    B, T, D = h.shape
    n = B * T
    f32 = jnp.float32
    hf = h.reshape(n, D)
    grp_logits = (hf @ w_grp).astype(f32) + b_grp.astype(f32)
    g_sel = jnp.argmax(grp_logits, axis=-1)
    p_grp = jnp.take_along_axis(jax.nn.softmax(grp_logits, axis=-1), g_sel[:, None], axis=1)
    exp_logits = ((hf @ w_exp).astype(f32) + b_exp.astype(f32)).reshape(n, N_GROUPS, EXPERTS_PER_GROUP)
    in_grp = jnp.take_along_axis(exp_logits, g_sel[:, None, None], axis=1)[:, 0]
    top_val, top_idx = lax.top_k(in_grp, EXPERT_TOPK)
    p_exp = jax.nn.softmax(top_val, axis=-1) * p_grp
    eid = g_sel[:, None] * EXPERTS_PER_GROUP + top_idx
    combine = jnp.sum(jax.nn.one_hot(eid, N_EXPERTS, dtype=f32) * p_exp[..., None], axis=1)
    y = jnp.zeros((n, D), f32)
    for gi in range(N_GROUPS):
        sl = slice(gi * EXPERTS_PER_GROUP, (gi + 1) * EXPERTS_PER_GROUP)
        hid = jax.nn.silu(jnp.einsum('nd,edf->nef', hf, w1[sl])) * jnp.einsum('nd,edf->nef', hf, w3[sl])
        hid = hid * combine[:, sl, None].astype(hid.dtype)
        y = y + jnp.einsum('nef,efd->nd', hid, w2[sl]).astype(f32)
    return y.reshape(B, T, D).astype(h.dtype)


def setup_inputs(seed: int = 0) -> dict:
    key = jax.random.key(seed)
    ks = iter(jax.random.split(key, 64))
    nrm = lambda shape, s: s * jax.random.normal(next(ks), shape, jnp.float32)
    gain = lambda shape: 1.0 + nrm(shape, 0.02)
    unif = lambda shape: jax.random.uniform(next(ks), shape, jnp.float32)
    L, V = DEPTH, DEPTH - 1
    decay_base = jnp.linspace(-6.5, -1.5, C_B, dtype=jnp.float32)
    return {
        'x': nrm((BATCH, SEQ, D_MODEL), 1.0),
        'meta_tokens': nrm((N_META, D_MODEL), 1.0),
        'norm_mix': gain((L, D_MODEL)),
        'w_in': nrm((L, D_MODEL, C_IN), D_MODEL ** -0.5),
        'mu_shift': unif((L, C_SHIFT)),
        'pool_w': nrm((L, N_POOL_GROUPS, POOL_GROUP_DIM, POOL_GROUP_DIM), POOL_GROUP_DIM ** -0.5),
        'pool_scale': gain((L, C_A)),
        'rwkv_w0': decay_base + nrm((L, C_B), 0.1),
        'rwkv_w2': nrm((L, RWKV_DECAY_RANK, C_B), 0.5 * RWKV_DECAY_RANK ** -0.5),
        'rwkv_a0': nrm((L, C_B), 0.1),
        'rwkv_a2': nrm((L, RWKV_A_RANK, C_B), 0.5 * RWKV_A_RANK ** -0.5),
        'rwkv_g2': nrm((L, RWKV_GATE_RANK, C_B), RWKV_GATE_RANK ** -0.5),
        'rwkv_k_k': 0.85 + nrm((L, C_B), 0.02),
        'rwkv_k_a': gain((L, C_B)),
        'rwkv_r_k': nrm((L, RWKV_HEADS, RWKV_HEAD_DIM), 0.1),
        'rwkv_ln_w': gain((L, C_B)),
        'rwkv_ln_b': nrm((L, C_B), 0.01),
        'vres_w_down': nrm((V, D_MODEL, RWKV_VRES_RANK), D_MODEL ** -0.5),
        'vres_mu': unif((V, RWKV_VRES_RANK)),
        'vres_v0': 1.0 + nrm((V, C_B), 0.1),
        'vres_v2': nrm((V, RWKV_VRES_RANK, C_B), 0.5 * RWKV_VRES_RANK ** -0.5),
        'mla_q_norm': gain((L, DSA_Q_RANK)),
        'mla_kv_norm': gain((L, DSA_KV_RANK)),
        'mla_w_uq': nrm((L, DSA_Q_RANK, DSA_HEADS * (DSA_NOPE_DIM + DSA_ROPE_DIM)), DSA_Q_RANK ** -0.5),
        'mla_w_uk': nrm((L, DSA_HEADS, DSA_KV_RANK, DSA_NOPE_DIM), DSA_NOPE_DIM ** -0.5),
        'mla_w_uv': nrm((L, DSA_HEADS, DSA_KV_RANK, DSA_V_DIM), DSA_KV_RANK ** -0.5),
        'idx_w_q': nrm((L, DSA_Q_RANK, IDX_HEADS * IDX_DIM), DSA_Q_RANK ** -0.5),
        'idx_k_norm_w': gain((L, IDX_DIM)),
        'idx_k_norm_b': nrm((L, IDX_DIM), 0.01),
        'w_proj_a': nrm((L, C_A, D_MODEL), C_A ** -0.5),
        'w_proj_b': nrm((L, C_B, D_MODEL), C_B ** -0.5),
        'w_proj_c': nrm((L, C_C, D_MODEL), C_C ** -0.5),
        'w_out': nrm((L, D_MODEL, D_MODEL), D_MODEL ** -0.5),
        'norm_ffn': gain((L, D_MODEL)),
        'router_w_group': nrm((L, D_MODEL, N_GROUPS), D_MODEL ** -0.5),
        'router_b_group': nrm((L, N_GROUPS), 0.01),
        'router_w_expert': nrm((L, D_MODEL, N_EXPERTS), D_MODEL ** -0.5),
        'router_b_expert': nrm((L, N_EXPERTS), 0.01),
        'expert_w1': nrm((L, N_EXPERTS, D_MODEL, EXPERT_HIDDEN), D_MODEL ** -0.5),
        'expert_w3': nrm((L, N_EXPERTS, D_MODEL, EXPERT_HIDDEN), D_MODEL ** -0.5),
        'expert_w2': nrm((L, N_EXPERTS, EXPERT_HIDDEN, D_MODEL), EXPERT_HIDDEN ** -0.5),
        'norm_final': gain((D_MODEL,)),
    }


def reference(x, meta_tokens, norm_mix, w_in, mu_shift, pool_w, pool_scale,
              rwkv_w0, rwkv_w2, rwkv_a0, rwkv_a2, rwkv_g2, rwkv_k_k, rwkv_k_a, rwkv_r_k,
              rwkv_ln_w, rwkv_ln_b, vres_w_down, vres_mu, vres_v0, vres_v2,
              mla_q_norm, mla_kv_norm, mla_w_uq, mla_w_uk, mla_w_uv,
              idx_w_q, idx_k_norm_w, idx_k_norm_b,
              w_proj_a, w_proj_b, w_proj_c, w_out, norm_ffn,
              router_w_group, router_b_group, router_w_expert, router_b_expert,
              expert_w1, expert_w3, expert_w2, norm_final):
    B, S, D = x.shape
    T = S + N_META
    topk = min(MAX_TOPK, S // 4)
    pos = jnp.arange(T, dtype=jnp.int32)
    h = jnp.concatenate([jnp.broadcast_to(meta_tokens[None].astype(x.dtype), (B, N_META, D)), x], axis=1)
    o0 = N_BRANCHES * D_MODEL
    o1 = o0 + C_A
    o2 = o1 + C_SHIFT
    o3 = o2 + C_DSA_IN
    v_first = None
    for l in range(DEPTH):
        u = rms_norm(h, norm_mix[l])
        w_cat = w_in[l] if l == 0 else jnp.concatenate([w_in[l], vres_w_down[l - 1]], axis=1)
        p = u @ w_cat
        gates = jax.nn.sigmoid(p[..., :o0].astype(jnp.float32)).reshape(B, T, N_BRANCHES, D_MODEL)
        pb = token_shift(p[..., o1:o2], mu_shift[l])
        vres = None if l == 0 else (token_shift(p[..., o3:], vres_mu[l - 1]), vres_v0[l - 1], vres_v2[l - 1])
        y_a = pool_mixer(p[..., o0:o1], pool_w[l], pool_scale[l])
        y_b, v_first = rwkv7_mixer(pb, vres, v_first, rwkv_w0[l], rwkv_w2[l], rwkv_a0[l], rwkv_a2[l],
                                   rwkv_g2[l], rwkv_k_k[l], rwkv_k_a[l], rwkv_r_k[l], rwkv_ln_w[l], rwkv_ln_b[l])
        y_c = dsa_mixer(p[..., o2:o3], pos, topk, mla_q_norm[l], mla_kv_norm[l], mla_w_uq[l], mla_w_uk[l],
                        mla_w_uv[l], idx_w_q[l], idx_k_norm_w[l], idx_k_norm_b[l])
        merged = (gates[:, :, 0] * (y_a @ w_proj_a[l]) + gates[:, :, 1] * (y_b @ w_proj_b[l])
                  + gates[:, :, 2] * (y_c @ w_proj_c[l]))
        h = h + merged.astype(h.dtype) @ w_out[l]
        h = h + hier_moe(rms_norm(h, norm_ffn[l]), router_w_group[l], router_b_group[l], router_w_expert[l],
                         router_b_expert[l], expert_w1[l], expert_w3[l], expert_w2[l])
    return rms_norm(h, norm_final)[:, N_META:]
```

```python
import functools

import jax
import jax.numpy as jnp
from jax import lax
from jax.experimental import pallas as pl
from jax.experimental.pallas import tpu as pltpu

F32 = jnp.float32
BF16 = jnp.bfloat16
I32 = jnp.int32

N_META = 16
EPS = 1e-6
ROPE_THETA = 10000.0
POOL_WINDOWS = (2, 4, 8, 16)
POOL_HALO = 16
HEAD_DIM = 64
GN_EPS = 64e-5
DECAY_RANK = 64
A_RANK = 64
GATE_RANK = 128
VRES_RANK = 32
DSA_HEADS = 16
NOPE = 64
ROPE = 32
V_DIM = 64
Q_RANK = 256
KV_RANK = 128
IDX_HEADS = 8
IDX_DIM = 64
MAX_TOPK = 256
N_GROUPS = 4
EPG = 4
N_EXPERTS = N_GROUPS * EPG

LANES = 128
VMEM_LIMIT = 56 * 1024 * 1024

CHUNK = 64
PAIR = 2 * HEAD_DIM
QB = 128
KB = 128
QCAT = KV_RANK + ROPE
IDX3 = 3 * IDX_DIM
INT_MIN = -2147483648
NEG_BIG = -1e30

NN = (((1,), (0,)), ((), ()))
NT = (((1,), (1,)), ((), ()))
TN = (((0,), (0,)), ((), ()))


def _dg(a, b, dims=NN):
    return lax.dot_general(a, b, dims, preferred_element_type=F32)


def _mm(a, b, dims=NN):
    return _dg(a.astype(BF16), b.astype(BF16), dims)


def _split(x):
    hi = x.astype(BF16)
    lo = (x - hi.astype(F32)).astype(BF16)
    return hi, lo


def _split3(x):
    a1 = x.astype(BF16)
    r1 = x - a1.astype(F32)
    a2 = r1.astype(BF16)
    a3 = (r1 - a2.astype(F32)).astype(BF16)
    return a1, a2, a3


def _mm3(a, b, dims=NN):
    ah, al = _split(a)
    bh, bl = _split(b)
    return _dg(ah, bh, dims) + (_dg(al, bh, dims) + _dg(ah, bl, dims))


def _mm_rhs_exact(a, m, dims=NN):
    ah, al = _split(a)
    return _dg(ah, m, dims) + _dg(al, m, dims)


def _cparams(sem):
    return pltpu.CompilerParams(dimension_semantics=sem, vmem_limit_bytes=VMEM_LIMIT)


def _row_block(n, target):
    best = None
    for d in range(16, min(n, target) + 1, 16):
        if n % d == 0:
            best = d
    assert best is not None, n
    return best


def _const(shape):
    nd = len(shape)
    return pl.BlockSpec(shape, lambda *_: (0,) * nd, pipeline_mode=pl.Buffered(1))


def _sigmoid(x):
    return 1.0 / (1.0 + jnp.exp(-x))


def _softplus(x):
    return jnp.maximum(x, 0.0) + jnp.log(1.0 + jnp.exp(-jnp.abs(x)))


def _rms(x, g):
    return x * lax.rsqrt(jnp.mean(x * x, axis=-1, keepdims=True) + EPS) * g


def _in_kernel(h_ref, g_ref, wg_ref, wp_ref, wr_ref, wd_ref, mu_ref, poolw_ref, pscale_ref,
               gates_ref, ya_ref, pb_ref, pc_ref, carry_pool, carry_r, *, tb):
    t = pl.program_id(1)

    @pl.when(t == 0)
    def _():
        carry_pool[...] = jnp.zeros_like(carry_pool)
        carry_r[...] = jnp.zeros_like(carry_r)

    u = _rms(h_ref[0], g_ref[...]).astype(BF16)

    gates_ref[0] = _sigmoid(_dg(u, wg_ref[...]))
    pc_ref[0] = _dg(u, wd_ref[...])

    p_r = _dg(u, wr_ref[...])
    row = lax.broadcasted_iota(I32, (tb, 1), 0)
    prev = jnp.where(row == 0, carry_r[7:8, :], pltpu.roll(p_r, 1, axis=0))
    pb_ref[0] = p_r + (prev - p_r) * mu_ref[...]
    carry_r[...] = p_r[tb - 8:, :]

    p_pool = _dg(u, wp_ref[...])
    ext = jnp.concatenate([carry_pool[...], p_pool], axis=0)
    carry_pool[...] = p_pool[tb - POOL_HALO:, :]
    tpos = (t * tb + row).astype(F32)
    gdim = p_pool.shape[1] // len(POOL_WINDOWS)
    ys = []
    for g, w in enumerate(POOL_WINDOWS):
        s, step = ext[:, g * gdim:(g + 1) * gdim], 1
        while step < w:
            s = s + pltpu.roll(s, step, axis=0)
            step *= 2
        cnt = jnp.minimum(tpos + 1.0, float(w))
        pooled = s[POOL_HALO:, :] / cnt - p_pool[:, g * gdim:(g + 1) * gdim]
        ys.append(_mm(pooled, poolw_ref[g]))
    ya_ref[0] = (jnp.concatenate(ys, axis=-1) * pscale_ref[...]).astype(BF16)


def _in_proj(h, g, wg, wp, wr, wd, mu, poolw, pscale, tb):
    B, Tp, D = h.shape
    blk = lambda c: pl.BlockSpec((1, tb, c), lambda b, t: (b, t, 0))
    widths = (wg.shape[1], wp.shape[1], wr.shape[1], wd.shape[1])
    return pl.pallas_call(
        functools.partial(_in_kernel, tb=tb),
        grid=(B, Tp // tb),
        in_specs=[blk(D)] + [_const(a.shape) for a in (g, wg, wp, wr, wd, mu, poolw, pscale)],
        out_specs=[blk(c) for c in widths],
        out_shape=[jax.ShapeDtypeStruct((B, Tp, widths[0]), F32),
                   jax.ShapeDtypeStruct((B, Tp, widths[1]), BF16),
                   jax.ShapeDtypeStruct((B, Tp, widths[2]), F32),
                   jax.ShapeDtypeStruct((B, Tp, widths[3]), F32)],
        scratch_shapes=[pltpu.VMEM((POOL_HALO, widths[1]), F32), pltpu.VMEM((8, widths[2]), F32)],
        compiler_params=_cparams(("arbitrary", "arbitrary")),
        name="in_proj",
    )(h, g, wg, wp, wr, wd, mu, poolw, pscale)


def _rwkv_prep_kernel(*refs, c, use_vres):
    if use_vres:
        (pb_ref, vf_ref, w0_ref, w2_ref, a0_ref, a2_ref, g2_ref, kk_ref, ka_ref, v0_ref, v2_ref,
         r_o, lw_o, k_o, v_o, a_o, b_o, g_o) = refs
    else:
        (pb_ref, w0_ref, w2_ref, a0_ref, a2_ref, g2_ref, kk_ref, ka_ref,
         r_o, lw_o, k_o, v_o, a_o, b_o, g_o) = refs
    o = 3 * c
    wd = pb_ref[0, :, o:o + DECAY_RANK]
    ad = pb_ref[0, :, o + DECAY_RANK:o + DECAY_RANK + A_RANK]
    gd = pb_ref[0, :, o + DECAY_RANK + A_RANK:o + DECAY_RANK + A_RANK + GATE_RANK]
    w_log = -_softplus(-(w0_ref[...] + _mm3(jnp.tanh(wd), w2_ref[...]))) - 0.5
    lw = -jnp.exp(w_log)
    a = _sigmoid(a0_ref[...] + _mm3(ad, a2_ref[...]))
    g = _mm3(_sigmoid(gd), g2_ref[...])
    if use_vres:
        vd = pb_ref[0, :, o + DECAY_RANK + A_RANK + GATE_RANK:o + DECAY_RANK + A_RANK + GATE_RANK + VRES_RANK]
        vmix = _sigmoid(v0_ref[...] + _mm3(vd, v2_ref[...]))
    ri = lax.broadcasted_iota(I32, (PAIR, PAIR), 0) // HEAD_DIM
    ci = lax.broadcasted_iota(I32, (PAIR, PAIR), 1) // HEAD_DIM
    ones_blk = (ri == ci).astype(BF16)
    for j in range(c // PAIR):
        sl = slice(j * PAIR, (j + 1) * PAIR)
        r = pb_ref[0, :, j * PAIR:(j + 1) * PAIR]
        k = pb_ref[0, :, c + j * PAIR:c + (j + 1) * PAIR]
        v = pb_ref[0, :, 2 * c + j * PAIR:2 * c + (j + 1) * PAIR]
        if use_vres:
            v = v + (vf_ref[0, j] - v) * vmix[:, sl]
        aj = a[:, sl]
        kk = k * kk_ref[:, sl]
        nrm = jnp.sqrt(_mm_rhs_exact(kk * kk, ones_blk))
        kk = kk / jnp.maximum(nrm, 1e-12)
        r_o[0, j] = r
        lw_o[0, j] = lw[:, sl]
        k_o[0, j] = k * (1.0 + (aj - 1.0) * ka_ref[:, sl])
        v_o[0, j] = v
        a_o[0, j] = -kk
        b_o[0, j] = kk * aj
        g_o[0, j] = g[:, sl]


def _rwkv_prep(pb, v_first, params, c, tb):
    B, Tp, NR = pb.shape
    npair = c // PAIR
    use_vres = v_first is not None
    pair_blk = pl.BlockSpec((1, npair, tb, PAIR), lambda b, t: (b, 0, t, 0))
    ins = [pb] + ([v_first] if use_vres else []) + list(params)
    in_specs = ([pl.BlockSpec((1, tb, NR), lambda b, t: (b, t, 0))] + ([pair_blk] if use_vres else [])
                + [_const(p.shape) for p in params])
    return pl.pallas_call(
        functools.partial(_rwkv_prep_kernel, c=c, use_vres=use_vres),
        grid=(B, Tp // tb),
        in_specs=in_specs,
        out_specs=[pair_blk] * 7,
        out_shape=[jax.ShapeDtypeStruct((B, npair, Tp, PAIR), F32)] * 7,
        compiler_params=_cparams(("arbitrary", "arbitrary")),
        name="rwkv_prep",
    )(*ins)


def _scan_kernel(r_ref, lw_ref, k_ref, v_ref, a_ref, b_ref, g_ref, lnw_ref, lnb_ref, rk_ref, y_ref,
                 *, n_chunks, unroll):
    C = CHUNK
    ri = lax.broadcasted_iota(I32, (PAIR, PAIR), 0)
    ci = lax.broadcasted_iota(I32, (PAIR, PAIR), 1)
    same = (ri // C) == (ci // C)
    strict = same & ((ri % C) > (ci % C))
    incl = same & ((ri % C) >= (ci % C))
    eye = (ri == ci).astype(F32)
    ones_blk = same.astype(BF16)
    tri = (lax.broadcasted_iota(I32, (C, C), 0) >= lax.broadcasted_iota(I32, (C, C), 1)).astype(BF16)
    lo = lax.broadcasted_iota(I32, (C, PAIR), 1) < HEAD_DIM
    lnw, lnb, rk = lnw_ref[0], lnb_ref[0], rk_ref[0]

    def blockdiag(x):
        return jnp.concatenate([jnp.where(lo, x, 0.0), jnp.where(lo, 0.0, x)], axis=0)

    def chunk_terms(ci_):
        sl = pl.ds(pl.multiple_of(ci_ * C, C), C)
        r, lw, k, v = r_ref[0, 0, sl, :], lw_ref[0, 0, sl, :], k_ref[0, 0, sl, :], v_ref[0, 0, sl, :]
        a, b = a_ref[0, 0, sl, :], b_ref[0, 0, sl, :]
        l1, l2, l3 = _split3(lw)
        cs = _dg(tri, l1) + (_dg(tri, l2) + _dg(tri, l3))
        p_in = jnp.exp(cs)
        p_inv = jnp.exp(-cs)
        a_d = blockdiag(a * jnp.exp(cs - lw))
        r_d = blockdiag(r * p_in)
        b_d = blockdiag(b * p_inv).astype(BF16)
        k_d = blockdiag(k * p_inv).astype(BF16)
        v_d = blockdiag(v).astype(BF16)
        m = _mm(jnp.concatenate([a_d, r_d], axis=0), jnp.concatenate([b_d, k_d], axis=0), NT)
        l_ab = jnp.where(strict, m[:PAIR, :PAIR], 0.0)
        l_ak = jnp.where(strict, m[:PAIR, PAIR:], 0.0)
        m_rb = jnp.where(incl, m[PAIR:, :PAIR], 0.0)
        m_rk = jnp.where(incl, m[PAIR:, PAIR:], 0.0)
        w = jnp.concatenate([a_d, _mm(l_ak, v_d)], axis=1)
        lp = l_ab
        step = 1
        while step < C:
            w = w + _mm(lp, w)
            step *= 2
            if step < C:
                lp = _mm(lp, lp)
        rbw = _mm(m_rb, w)
        r2 = r_d + rbw[:, :PAIR]
        o0 = rbw[:, PAIR:] + _mm(m_rk, v_d)
        tw = _mm(w, b_d, TN)
        plast = p_in[C - 1:C, :]
        gmat = (eye + tw[:PAIR]) * plast
        hmat = (tw[PAIR:] + _mm(v_d, k_d, TN)) * plast
        return r2, o0, gmat, hmat

    def body(it, s):
        terms = [chunk_terms(it * unroll + u) for u in range(unroll)]
        for u in range(unroll):
            r2, o0, gmat, hmat = terms[u]
            sl = pl.ds(pl.multiple_of((it * unroll + u) * C, C), C)
            od = _mm(r2, s, NT) + o0
            s = _mm3(s, gmat) + hmat
            o = od[:C] + od[C:]
            mu = _mm_rhs_exact(o, ones_blk) * (1.0 / HEAD_DIM)
            d = o - mu
            var = _mm_rhs_exact(d * d, ones_blk) * (1.0 / HEAD_DIM)
            on = d * lax.rsqrt(var + GN_EPS) * lnw + lnb
            r, k, v = r_ref[0, 0, sl, :], k_ref[0, 0, sl, :], v_ref[0, 0, sl, :]
            bonus = _mm_rhs_exact(r * k * rk, ones_blk) * v
            y_ref[0, 0, sl, :] = ((on + bonus) * g_ref[0, 0, sl, :]).astype(BF16)
        return s

    lax.fori_loop(0, n_chunks // unroll, body, jnp.zeros((PAIR, PAIR), F32))


def _rwkv_scan(r, lw, k, v, a, b, g, lnw, lnb, rk):
    B, npair, Tp, _ = r.shape
    n_chunks = Tp // CHUNK
    unroll = 2 if n_chunks % 2 == 0 else 1
    seq = pl.BlockSpec((1, 1, Tp, PAIR), lambda bi, j: (bi, j, 0, 0))
    par = pl.BlockSpec((1, 1, PAIR), lambda bi, j: (j, 0, 0))
    return pl.pallas_call(
        functools.partial(_scan_kernel, n_chunks=n_chunks, unroll=unroll),
        grid=(B, npair),
        in_specs=[seq] * 7 + [par] * 3,
        out_specs=seq,
        out_shape=jax.ShapeDtypeStruct((B, npair, Tp, PAIR), BF16),
        compiler_params=_cparams(("arbitrary", "arbitrary")),
        name="rwkv_scan",
    )(r, lw, k, v, a, b, g, lnw, lnb, rk)


def _swap16(x):
    n = x.shape[-1]
    lane = lax.broadcasted_iota(I32, x.shape, x.ndim - 1)
    return jnp.where((lane % 32) < 16, pltpu.roll(x, n - 16, axis=x.ndim - 1), pltpu.roll(x, 16, axis=x.ndim - 1))


def _dsa_prep_kernel(pc_ref, cq_t, sq_t, ci_t, si_t, ck_t, sk_t, gq_ref, gkv_ref, wn_ref, wr_ref, wuk_ref,
                     wqi_ref, lnw_ref, lnb_ref, qcat_o, kcat_o, qidx_o, kidx_o, wh_o, *, scale):
    cq = _rms(pc_ref[0, :, :Q_RANK], gq_ref[...])
    ckv = _rms(pc_ref[0, :, Q_RANK:Q_RANK + KV_RANK], gkv_ref[...])
    slab = pc_ref[0, :, Q_RANK + KV_RANK:Q_RANK + KV_RANK + LANES]
    lane = lax.broadcasted_iota(I32, slab.shape, 1)
    in_idx = (lane >= ROPE) & (lane < ROPE + IDX_DIM)
    mean = jnp.sum(jnp.where(in_idx, slab, 0.0), axis=-1, keepdims=True) * (1.0 / IDX_DIM)
    dev = jnp.where(in_idx, slab - mean, 0.0)
    var = jnp.sum(dev * dev, axis=-1, keepdims=True) * (1.0 / IDX_DIM)
    kn = dev * lax.rsqrt(var + EPS) * lnw_ref[...] + lnb_ref[...]
    kslab = jnp.where(lane < ROPE, slab, kn)
    kslab = kslab * ck_t[...] + _swap16(kslab) * sk_t[...]
    kcat_o[0] = jnp.concatenate([ckv, kslab[:, :ROPE]], axis=-1).astype(BF16)
    khi, klo = _split(kslab[:, ROPE:ROPE + IDX_DIM])
    kidx_o[0] = jnp.concatenate([khi, khi, klo], axis=-1)
    wh_o[0] = slab[:, ROPE + IDX_DIM:ROPE + IDX_DIM + IDX_HEADS] * (IDX_HEADS ** -0.5 * IDX_DIM ** -0.5)

    q_nope = _mm(cq, wn_ref[...])
    q_rope = _mm(cq, wr_ref[...])
    q_rope = q_rope * cq_t[...] + _swap16(q_rope) * sq_t[...]
    q_idx = _mm3(cq, wqi_ref[...])
    q_idx = q_idx * ci_t[...] + _swap16(q_idx) * si_t[...]
    for hd in range(DSA_HEADS):
        q_lat = _mm(q_nope[:, hd * NOPE:(hd + 1) * NOPE], wuk_ref[hd], NT)
        qc = jnp.concatenate([q_lat, q_rope[:, hd * ROPE:(hd + 1) * ROPE]], axis=-1) * scale
        qcat_o[0, hd] = qc.astype(BF16)
    for hd in range(IDX_HEADS):
        qhi, qlo = _split(q_idx[:, hd * IDX_DIM:(hd + 1) * IDX_DIM])
        qidx_o[0, hd] = jnp.concatenate([qhi, qlo, qhi], axis=-1)


def _dsa_prep(pc, tables, params, tb):
    B, Tp, ND = pc.shape
    row = lambda c: pl.BlockSpec((1, tb, c), lambda b, t: (b, t, 0))
    tab = lambda a: pl.BlockSpec((tb, a.shape[1]), lambda b, t: (t, 0))
    head = lambda n, c: pl.BlockSpec((1, n, tb, c), lambda b, t: (b, 0, t, 0))
    scale = float((NOPE + ROPE) ** -0.5)
    return pl.pallas_call(
        functools.partial(_dsa_prep_kernel, scale=scale),
        grid=(B, Tp // tb),
        in_specs=[row(ND)] + [tab(a) for a in tables] + [_const(p.shape) for p in params],
        out_specs=[head(DSA_HEADS, QCAT), row(QCAT), head(IDX_HEADS, IDX3), row(IDX3), row(IDX_HEADS)],
        out_shape=[jax.ShapeDtypeStruct((B, DSA_HEADS, Tp, QCAT), BF16),
                   jax.ShapeDtypeStruct((B, Tp, QCAT), BF16),
                   jax.ShapeDtypeStruct((B, IDX_HEADS, Tp, IDX3), BF16),
                   jax.ShapeDtypeStruct((B, Tp, IDX3), BF16),
                   jax.ShapeDtypeStruct((B, Tp, IDX_HEADS), F32)],
        compiler_params=_cparams(("arbitrary", "arbitrary")),
        name="dsa_prep",
    )(pc, *tables, *params)


def _dsa_kernel(qcat_ref, qidx_ref, wh_ref, kcat_ref, kidx_ref, wuv_ref, y_ref,
                keys_ref, m_ref, l_ref, acc_ref, *, topk, pos_bits):
    i = pl.program_id(1)
    nj = (i * QB + QB + KB - 1) // KB
    qpos = i * QB + lax.broadcasted_iota(I32, (QB, KB), 0)
    lane = lax.broadcasted_iota(I32, (QB, KB), 1)
    q_idx = qidx_ref[0].reshape(IDX_HEADS * QB, IDX3)
    w_head = wh_ref[0]

    def score_chunk(j, carry):
        kc = kidx_ref[0, pl.ds(pl.multiple_of(j * KB, KB), KB), :]
        s = jnp.maximum(_dg(q_idx, kc, NT), 0.0)
        tot = s[:QB] * w_head[:, 0:1]
        for hd in range(1, IDX_HEADS):
            tot = tot + s[hd * QB:(hd + 1) * QB] * w_head[:, hd:hd + 1]
        tot = jnp.where(tot == 0.0, 0.0, tot)
        bits = pltpu.bitcast(tot, I32)
        key = jnp.where(bits < 0, bits ^ 0x7FFFFFFF, bits)
        keys_ref[j] = jnp.where(j * KB + lane <= qpos, key, INT_MIN)
        return carry

    lax.fori_loop(0, nj, score_chunk, 0)

    def count(pred):
        def step(j, acc):
            return acc + pred(keys_ref[j], j).astype(I32)
        acc = lax.fori_loop(0, nj, step, jnp.zeros((QB, KB), I32))
        return jnp.sum(acc, axis=-1, keepdims=True)

    v0 = jnp.where(count(lambda kk, j: kk >= 0) >= topk, 0, INT_MIN).astype(I32)

    def vbit(t, v):
        cand = v + lax.shift_left(jnp.int32(1), 30 - t)
        return jnp.where(count(lambda kk, j: kk >= cand) >= topk, cand, v)

    vth = lax.fori_loop(0, 31, vbit, v0)
    need = topk - count(lambda kk, j: kk > vth)

    def jbit(t, jv):
        cand = jv + lax.shift_left(jnp.int32(1), pos_bits - 1 - t)
        c = count(lambda kk, j: (kk == vth) & (j * KB + lane < cand))
        return jnp.where(c < need, cand, jv)

    jth = lax.fori_loop(0, pos_bits, jbit, jnp.zeros((QB, 1), I32))

    m_ref[...] = jnp.full_like(m_ref, NEG_BIG)
    l_ref[...] = jnp.zeros_like(l_ref)
    acc_ref[...] = jnp.zeros_like(acc_ref)
    q_all = qcat_ref[0].reshape(DSA_HEADS * QB, QCAT)

    def attend(j, carry):
        kc = kcat_ref[0, pl.ds(pl.multiple_of(j * KB, KB), KB), :]
        key = keys_ref[j]
        kpos = j * KB + lane
        sel = ((key > vth) | ((key == vth) & (kpos <= jth))) & (kpos <= qpos)
        bias = jnp.where(sel, 0.0, -jnp.inf)
        s = _dg(q_all, kc, NT).reshape(DSA_HEADS, QB, KB) + bias[None]
        m_prev = m_ref[...]
        m_new = jnp.maximum(m_prev, jnp.max(s, axis=-1, keepdims=True))
        alpha = jnp.exp(m_prev - m_new)
        p = jnp.exp(s - m_new)
        l_ref[...] = l_ref[...] * alpha + jnp.sum(p, axis=-1, keepdims=True)
        pv = _dg(p.reshape(DSA_HEADS * QB, KB).astype(BF16), kc[:, :KV_RANK])
        acc_ref[...] = acc_ref[...] * alpha + pv.reshape(DSA_HEADS, QB, KV_RANK)
        m_ref[...] = m_new
        return carry

    lax.fori_loop(0, nj, attend, 0)
    o_lat = acc_ref[...] / l_ref[...]
    y_ref[0] = jnp.concatenate([_mm(o_lat[hd], wuv_ref[hd]) for hd in range(DSA_HEADS)], axis=-1).astype(BF16)


def _dsa(qcat, kcat, qidx, kidx, wh, wuv, topk):
    B, _, Tp, _ = qcat.shape
    nk = Tp // KB
    pos_bits = max(1, (Tp - 1).bit_length())
    qh = lambda n, c: pl.BlockSpec((1, n, QB, c), lambda b, i: (b, 0, i, 0))
    whole = lambda c: pl.BlockSpec((1, Tp, c), lambda b, i: (b, 0, 0))
    return pl.pallas_call(
        functools.partial(_dsa_kernel, topk=topk, pos_bits=pos_bits),
        grid=(B, Tp // QB),
        in_specs=[qh(DSA_HEADS, QCAT), qh(IDX_HEADS, IDX3), pl.BlockSpec((1, QB, IDX_HEADS), lambda b, i: (b, i, 0)),
                  whole(QCAT), whole(IDX3), _const(wuv.shape)],
        out_specs=pl.BlockSpec((1, QB, DSA_HEADS * V_DIM), lambda b, i: (b, i, 0)),
        out_shape=jax.ShapeDtypeStruct((B, Tp, DSA_HEADS * V_DIM), BF16),
        scratch_shapes=[pltpu.VMEM((nk, QB, KB), I32),
                        pltpu.VMEM((DSA_HEADS, QB, 1), F32),
                        pltpu.VMEM((DSA_HEADS, QB, 1), F32),
                        pltpu.VMEM((DSA_HEADS, QB, KV_RANK), F32)],
        compiler_params=_cparams(("arbitrary", "arbitrary")),
        name="dsa_attn",
    )(qcat, qidx, wh, kcat, kidx, wuv)


def _merge_kernel(h_ref, gates_ref, ya_ref, yb_ref, yc_ref, pa_ref, pb_ref, pc_ref, wo_ref, o_ref, *, d):
    za = _dg(ya_ref[0], pa_ref[...])
    zb = _dg(yb_ref[0, 0], pb_ref[0:PAIR, :])
    for j in range(1, yb_ref.shape[1]):
        zb = zb + _dg(yb_ref[0, j], pb_ref[j * PAIR:(j + 1) * PAIR, :])
    zc = _dg(yc_ref[0], pc_ref[...])
    merged = gates_ref[0, :, :d] * za + gates_ref[0, :, d:2 * d] * zb + gates_ref[0, :, 2 * d:] * zc
    o_ref[0] = h_ref[0] + _mm(merged, wo_ref[...])


def _merge(h, gates, ya, yb, yc, pa, pb, pc, wo, tb):
    B, Tp, D = h.shape
    row = lambda c: pl.BlockSpec((1, tb, c), lambda b, t: (b, t, 0))
    return pl.pallas_call(
        functools.partial(_merge_kernel, d=D),
        grid=(B, Tp // tb),
        in_specs=[row(D), row(gates.shape[2]), row(ya.shape[2]),
                  pl.BlockSpec((1, yb.shape[1], tb, PAIR), lambda b, t: (b, 0, t, 0)), row(yc.shape[2])]
                 + [_const(w.shape) for w in (pa, pb, pc, wo)],
        out_specs=row(D),
        out_shape=jax.ShapeDtypeStruct((B, Tp, D), F32),
        compiler_params=_cparams(("arbitrary", "arbitrary")),
        name="merge",
    )(h, gates, ya, yb, yc, pa, pb, pc, wo)


def _first_index_of_max(x, valid, idx):
    big = jnp.int32(1 << 20)
    mx = jnp.max(jnp.where(valid, x, -jnp.inf), axis=-1, keepdims=True)
    first = jnp.min(jnp.where(valid & (x == mx), idx, big), axis=-1, keepdims=True)
    return mx, first


def _moe_kernel(h_ref, gn_ref, wg_ref, bg_ref, we_ref, be_ref, w13_ref, w2_ref, gf_ref, o_ref,
                hn_ref, comb_ref, *, hidden, final):
    e = pl.program_id(1)

    @pl.when(e == 0)
    def _():
        h = h_ref[...]
        hn = _rms(h, gn_ref[...])
        hn_ref[...] = hn.astype(BF16)
        gl = _mm3(hn, wg_ref[...]) + bg_ref[...]
        el = _mm3(hn, we_ref[...]) + be_ref[...]
        gidx = lax.broadcasted_iota(I32, gl.shape, 1)
        gmax, gsel = _first_index_of_max(gl, gidx >= 0, gidx)
        p_grp = 1.0 / jnp.sum(jnp.exp(gl - gmax), axis=-1, keepdims=True)
        eidx = lax.broadcasted_iota(I32, el.shape, 1)
        in_grp = (eidx // EPG) == gsel
        t1, i1 = _first_index_of_max(el, in_grp, eidx)
        t2, i2 = _first_index_of_max(el, in_grp & (eidx != i1), eidx)
        e2 = jnp.exp(t2 - t1)
        w1 = p_grp / (1.0 + e2)
        w2 = p_grp * e2 / (1.0 + e2)
        comb_ref[...] = jnp.where(eidx == i1, w1, 0.0) + jnp.where(eidx == i2, w2, 0.0)
        o_ref[...] = h

    hn = hn_ref[...]
    eidx = lax.broadcasted_iota(I32, comb_ref.shape, 1)
    ce = jnp.sum(jnp.where(eidx == e, comb_ref[...], 0.0), axis=-1, keepdims=True)
    x13 = _dg(hn, w13_ref[0])
    x1, x3 = x13[:, :hidden], x13[:, hidden:]
    hid = x1 * _sigmoid(x1) * x3 * ce
    o_ref[...] += _mm(hid, w2_ref[0])

    if final:
        @pl.when(e == pl.num_programs(1) - 1)
        def _():
            o_ref[...] = _rms(o_ref[...], gf_ref[...])


def _moe(h2, gn, wg, bg, we, be, w13, w2, gf, final, tb):
    n, D = h2.shape
    E, _, H2 = w13.shape
    row = pl.BlockSpec((tb, D), lambda r, e: (r, 0))
    return pl.pallas_call(
        functools.partial(_moe_kernel, hidden=H2 // 2, final=final),
        grid=(n // tb, E),
        in_specs=[row] + [_const(a.shape) for a in (gn, wg, bg, we, be)]
                 + [pl.BlockSpec((1, D, H2), lambda r, e: (e, 0, 0)),
                    pl.BlockSpec((1, H2 // 2, D), lambda r, e: (e, 0, 0)), _const(gf.shape)],
        out_specs=row,
        out_shape=jax.ShapeDtypeStruct((n, D), F32),
        scratch_shapes=[pltpu.VMEM((tb, D), BF16), pltpu.VMEM((tb, E), F32)],
        compiler_params=_cparams(("arbitrary", "arbitrary")),
        name="moe",
    )(h2, gn, wg, bg, we, be, w13, w2, gf)


def _rope_tables(tp):
    inv = ROPE_THETA ** (-jnp.arange(0, ROPE, 2, dtype=F32) / ROPE)
    ang = jnp.arange(tp, dtype=F32)[:, None] * inv[None, :]
    cos, sin = jnp.cos(ang), jnp.sin(ang)
    c32 = jnp.concatenate([cos, cos], axis=-1)
    s32 = jnp.concatenate([-sin, sin], axis=-1)
    one32, zero32 = jnp.ones_like(c32), jnp.zeros_like(c32)
    cq, sq = jnp.tile(c32, (1, DSA_HEADS)), jnp.tile(s32, (1, DSA_HEADS))
    ci = jnp.tile(jnp.concatenate([c32, one32], axis=-1), (1, IDX_HEADS))
    si = jnp.tile(jnp.concatenate([s32, zero32], axis=-1), (1, IDX_HEADS))
    ck = jnp.concatenate([c32, c32, one32, one32], axis=-1)
    sk = jnp.concatenate([s32, s32, zero32, zero32], axis=-1)
    return cq, sq, ci, si, ck, sk


def _pad_cols(w, n):
    return jnp.pad(w, ((0, 0), (0, n - w.shape[1])))


def kernel(x, meta_tokens, norm_mix, w_in, mu_shift, pool_w, pool_scale, rwkv_w0, rwkv_w2, rwkv_a0, rwkv_a2, rwkv_g2, rwkv_k_k, rwkv_k_a, rwkv_r_k, rwkv_ln_w, rwkv_ln_b, vres_w_down, vres_mu, vres_v0, vres_v2, mla_q_norm, mla_kv_norm, mla_w_uq, mla_w_uk, mla_w_uv, idx_w_q, idx_k_norm_w, idx_k_norm_b, w_proj_a, w_proj_b, w_proj_c, w_out, norm_ffn, router_w_group, router_b_group, router_w_expert, router_b_expert, expert_w1, expert_w3, expert_w2, norm_final):
    B, S, D = x.shape
    depth = w_in.shape[0]
    T = S + N_META
    Tp = -(-T // LANES) * LANES
    topk = min(MAX_TOPK, S // 4)
    c_a = pool_scale.shape[1]
    c_b = rwkv_w0.shape[1]
    c_shift = mu_shift.shape[1]
    npair = c_b // PAIR
    o0 = 3 * D
    o1 = o0 + c_a
    o2 = o1 + c_shift
    c_dsa = Q_RANK + KV_RANK + ROPE + IDX_DIM + IDX_HEADS
    o3 = o2 + c_dsa
    nr = -(-(c_shift + VRES_RANK) // LANES) * LANES
    nd = -(-c_dsa // LANES) * LANES
    tb_in = _row_block(Tp, 288)
    tb_row = _row_block(Tp, 576)
    tb_moe = _row_block(B * Tp, 1152)
    row1 = lambda v: v.reshape(1, -1)

    h = jnp.concatenate([jnp.broadcast_to(meta_tokens[None].astype(x.dtype), (B, N_META, D)), x,
                         jnp.zeros((B, Tp - T, D), x.dtype)], axis=1)
    tables = _rope_tables(Tp)
    v_first = None
    for l in range(depth):
        w_l = w_in[l]
        w_r = w_l[:, o1:o2]
        mu = mu_shift[l]
        if l > 0:
            w_r = jnp.concatenate([w_r, vres_w_down[l - 1]], axis=1)
            mu = jnp.concatenate([mu, vres_mu[l - 1]])
        gates, ya, pb, pc = _in_proj(
            h, row1(norm_mix[l]), w_l[:, :o0].astype(BF16), w_l[:, o0:o1].astype(BF16),
            _pad_cols(w_r, nr).astype(BF16), _pad_cols(w_l[:, o2:o3], nd).astype(BF16),
            _pad_cols(row1(mu), nr), pool_w[l].astype(BF16), row1(pool_scale[l]), tb_in)

        prep_params = [row1(rwkv_w0[l]), rwkv_w2[l], row1(rwkv_a0[l]), rwkv_a2[l], rwkv_g2[l],
                       row1(rwkv_k_k[l]), row1(rwkv_k_a[l])]
        if l > 0:
            prep_params += [row1(vres_v0[l - 1]), vres_v2[l - 1]]
        r_, lw_, k_, v_, a_, b_, g_ = _rwkv_prep(pb, v_first if l > 0 else None, prep_params, c_b, tb_in)
        if l == 0:
            v_first = v_
        yb = _rwkv_scan(r_, lw_, k_, v_, a_, b_, g_, rwkv_ln_w[l].reshape(npair, 1, PAIR),
                        rwkv_ln_b[l].reshape(npair, 1, PAIR), rwkv_r_k[l].reshape(npair, 1, PAIR))

        w_uq = mla_w_uq[l].reshape(Q_RANK, DSA_HEADS, NOPE + ROPE)
        idx_slab = lambda v: jnp.pad(v, (ROPE, LANES - ROPE - IDX_DIM)).reshape(1, LANES)
        dsa_params = [row1(mla_q_norm[l]), row1(mla_kv_norm[l]),
                      w_uq[:, :, :NOPE].reshape(Q_RANK, DSA_HEADS * NOPE).astype(BF16),
                      w_uq[:, :, NOPE:].reshape(Q_RANK, DSA_HEADS * ROPE).astype(BF16),
                      mla_w_uk[l].astype(BF16), idx_w_q[l], idx_slab(idx_k_norm_w[l]), idx_slab(idx_k_norm_b[l])]
        qcat, kcat, qidx, kidx, wh = _dsa_prep(pc, tables, dsa_params, tb_in)
        yc = _dsa(qcat, kcat, qidx, kidx, wh, mla_w_uv[l].astype(BF16), topk)

        h = _merge(h, gates, ya, yb, yc, w_proj_a[l].astype(BF16), w_proj_b[l].astype(BF16),
                   w_proj_c[l].astype(BF16), w_out[l].astype(BF16), tb_row)

        w13 = jnp.concatenate([expert_w1[l], expert_w3[l]], axis=-1).astype(BF16)
        h = _moe(h.reshape(B * Tp, D), row1(norm_ffn[l]), router_w_group[l], row1(router_b_group[l]),
                 router_w_expert[l], row1(router_b_expert[l]), w13, expert_w2[l].astype(BF16),
                 row1(norm_final), l == depth - 1, tb_moe).reshape(B, Tp, D)
    return h[:, N_META:T]
```

```python
import functools
import math

import jax
import jax.numpy as jnp
from jax import lax
from jax.experimental import pallas as pl
from jax.experimental.pallas import tpu as pltpu

F32 = jnp.float32
BF16 = jnp.bfloat16
I32 = jnp.int32

N_META = 16
EPS = 1e-6
ROPE_THETA = 10000.0
POOL_WINDOWS = (2, 4, 8, 16)
POOL_HALO = 16
HEAD_DIM = 64
GN_EPS = 64e-5
DECAY_RANK = 64
A_RANK = 64
GATE_RANK = 128
VRES_RANK = 32
DSA_HEADS = 16
NOPE = 64
ROPE = 32
V_DIM = 64
Q_RANK = 256
KV_RANK = 128
IDX_HEADS = 8
IDX_DIM = 64
MAX_TOPK = 256
N_GROUPS = 4
EPG = 4
N_EXPERTS = N_GROUPS * EPG

LANES = 128
VMEM_LIMIT = 56 * 1024 * 1024

CHUNK = 64
GRP_HEADS = 4
GRP = GRP_HEADS * HEAD_DIM
SCAN_CHUNKS = 2
QB = 128
KB = 512
ATT_GROUPS = 4
QCAT = KV_RANK + ROPE
KCAT = 2 * LANES
IDX3 = 3 * IDX_DIM
INT_MIN = -2147483648
NEG_BIG = -1e30

NN = (((1,), (0,)), ((), ()))
NT = (((1,), (1,)), ((), ()))
TN = (((0,), (0,)), ((), ()))


def _dg(a, b, dims=NN):
    return lax.dot_general(a, b, dims, preferred_element_type=F32)


def _mm(a, b, dims=NN):
    return _dg(a.astype(BF16), b.astype(BF16), dims)


def _split(x):
    hi = x.astype(BF16)
    lo = (x - hi.astype(F32)).astype(BF16)
    return hi, lo


def _split3(x):
    a1 = x.astype(BF16)
    r1 = x - a1.astype(F32)
    a2 = r1.astype(BF16)
    a3 = (r1 - a2.astype(F32)).astype(BF16)
    return a1, a2, a3


def _mm3(a, b, dims=NN):
    ah, al = _split(a)
    bh, bl = _split(b)
    return _dg(ah, bh, dims) + (_dg(al, bh, dims) + _dg(ah, bl, dims))


def _mm_rhs_exact(a, m, dims=NN):
    ah, al = _split(a)
    return _dg(ah, m, dims) + _dg(al, m, dims)


def _cparams(sem):
    return pltpu.CompilerParams(dimension_semantics=sem, vmem_limit_bytes=VMEM_LIMIT)


def _row_block(n, target):
    best = None
    for d in range(16, min(n, target) + 1, 16):
        if n % d == 0:
            best = d
    assert best is not None, n
    return best


def _const(shape):
    nd = len(shape)
    return pl.BlockSpec(shape, lambda *_: (0,) * nd, pipeline_mode=pl.Buffered(1))


def _sigmoid(x):
    return 1.0 / (1.0 + jnp.exp(-x))


def _softplus(x):
    return jnp.maximum(x, 0.0) + jnp.log(1.0 + jnp.exp(-jnp.abs(x)))


def _rms(x, g):
    return x * lax.rsqrt(jnp.mean(x * x, axis=-1, keepdims=True) + EPS) * g


def _in_kernel(h_ref, g_ref, wg_ref, wp_ref, wr_ref, wd_ref, mu_ref, poolw_ref, pscale_ref,
               gates_ref, ya_ref, pb_ref, pc_ref, carry_pool, carry_r, *, tb):
    t = pl.program_id(1)

    @pl.when(t == 0)
    def _():
        carry_pool[...] = jnp.zeros_like(carry_pool)
        carry_r[...] = jnp.zeros_like(carry_r)

    u = _rms(h_ref[0], g_ref[...]).astype(BF16)

    gates_ref[0] = _sigmoid(_dg(u, wg_ref[...]))
    pc_ref[0] = _dg(u, wd_ref[...])

    p_r = _dg(u, wr_ref[...])
    row = lax.broadcasted_iota(I32, (tb, 1), 0)
    prev = jnp.where(row == 0, carry_r[7:8, :], pltpu.roll(p_r, 1, axis=0))
    pb_ref[0] = p_r + (prev - p_r) * mu_ref[...]
    carry_r[...] = p_r[tb - 8:, :]

    p_pool = _dg(u, wp_ref[...])
    ext = jnp.concatenate([carry_pool[...], p_pool], axis=0)
    carry_pool[...] = p_pool[tb - POOL_HALO:, :]
    tpos = (t * tb + row).astype(F32)
    gdim = p_pool.shape[1] // len(POOL_WINDOWS)
    ys = []
    for g, w in enumerate(POOL_WINDOWS):
        s, step = ext[:, g * gdim:(g + 1) * gdim], 1
        while step < w:
            s = s + pltpu.roll(s, step, axis=0)
            step *= 2
        cnt = jnp.minimum(tpos + 1.0, float(w))
        pooled = s[POOL_HALO:, :] / cnt - p_pool[:, g * gdim:(g + 1) * gdim]
        ys.append(_mm(pooled, poolw_ref[g]))
    ya_ref[0] = (jnp.concatenate(ys, axis=-1) * pscale_ref[...]).astype(BF16)


def _in_proj(h, g, wg, wp, wr, wd, mu, poolw, pscale, tb):
    B, Tp, D = h.shape
    blk = lambda c: pl.BlockSpec((1, tb, c), lambda b, t: (b, t, 0))
    widths = (wg.shape[1], wp.shape[1], wr.shape[1], wd.shape[1])
    return pl.pallas_call(
        functools.partial(_in_kernel, tb=tb),
        grid=(B, Tp // tb),
        in_specs=[blk(D)] + [_const(a.shape) for a in (g, wg, wp, wr, wd, mu, poolw, pscale)],
        out_specs=[blk(c) for c in widths],
        out_shape=[jax.ShapeDtypeStruct((B, Tp, widths[0]), F32),
                   jax.ShapeDtypeStruct((B, Tp, widths[1]), BF16),
                   jax.ShapeDtypeStruct((B, Tp, widths[2]), F32),
                   jax.ShapeDtypeStruct((B, Tp, widths[3]), F32)],
        scratch_shapes=[pltpu.VMEM((POOL_HALO, widths[1]), F32), pltpu.VMEM((8, widths[2]), F32)],
        compiler_params=_cparams(("arbitrary", "arbitrary")),
        name="in_proj",
    )(h, g, wg, wp, wr, wd, mu, poolw, pscale)


def _rwkv_prep_kernel(*refs, c, use_vres):
    if use_vres:
        (pb_ref, vf_ref, w0_ref, w2_ref, a0_ref, a2_ref, g2_ref, kk_ref, ka_ref, v0_ref, v2_ref,
         r_o, lw_o, k_o, v_o, a_o, b_o, g_o) = refs
    else:
        (pb_ref, w0_ref, w2_ref, a0_ref, a2_ref, g2_ref, kk_ref, ka_ref,
         r_o, lw_o, k_o, v_o, a_o, b_o, g_o) = refs
    o = 3 * c
    wd = pb_ref[0, :, o:o + DECAY_RANK]
    ad = pb_ref[0, :, o + DECAY_RANK:o + DECAY_RANK + A_RANK]
    gd = pb_ref[0, :, o + DECAY_RANK + A_RANK:o + DECAY_RANK + A_RANK + GATE_RANK]
    w_log = -_softplus(-(w0_ref[...] + _mm3(jnp.tanh(wd), w2_ref[...]))) - 0.5
    lw = -jnp.exp(w_log)
    a = _sigmoid(a0_ref[...] + _mm3(ad, a2_ref[...]))
    g = _mm3(_sigmoid(gd), g2_ref[...])
    if use_vres:
        vd = pb_ref[0, :, o + DECAY_RANK + A_RANK + GATE_RANK:o + DECAY_RANK + A_RANK + GATE_RANK + VRES_RANK]
        vmix = _sigmoid(v0_ref[...] + _mm3(vd, v2_ref[...]))
    ri = lax.broadcasted_iota(I32, (GRP, GRP), 0) // HEAD_DIM
    ci = lax.broadcasted_iota(I32, (GRP, GRP), 1) // HEAD_DIM
    ones_blk = (ri == ci).astype(BF16)
    for j in range(c // GRP):
        sl = slice(j * GRP, (j + 1) * GRP)
        r = pb_ref[0, :, j * GRP:(j + 1) * GRP]
        k = pb_ref[0, :, c + j * GRP:c + (j + 1) * GRP]
        v = pb_ref[0, :, 2 * c + j * GRP:2 * c + (j + 1) * GRP]
        if use_vres:
            v = v + (vf_ref[0, j] - v) * vmix[:, sl]
        aj = a[:, sl]
        kk = k * kk_ref[:, sl]
        nrm = jnp.sqrt(_mm_rhs_exact(kk * kk, ones_blk))
        kk = kk / jnp.maximum(nrm, 1e-12)
        r_o[0, j] = r
        lw_o[0, j] = lw[:, sl]
        k_o[0, j] = k * (1.0 + (aj - 1.0) * ka_ref[:, sl])
        v_o[0, j] = v
        a_o[0, j] = -kk
        b_o[0, j] = kk * aj
        g_o[0, j] = g[:, sl]


def _rwkv_prep(pb, v_first, params, c, tb):
    B, Tp, NR = pb.shape
    npair = c // GRP
    use_vres = v_first is not None
    pair_blk = pl.BlockSpec((1, npair, tb, GRP), lambda b, t: (b, 0, t, 0))
    ins = [pb] + ([v_first] if use_vres else []) + list(params)
    in_specs = ([pl.BlockSpec((1, tb, NR), lambda b, t: (b, t, 0))] + ([pair_blk] if use_vres else [])
                + [_const(p.shape) for p in params])
    return pl.pallas_call(
        functools.partial(_rwkv_prep_kernel, c=c, use_vres=use_vres),
        grid=(B, Tp // tb),
        in_specs=in_specs,
        out_specs=[pair_blk] * 7,
        out_shape=[jax.ShapeDtypeStruct((B, npair, Tp, GRP), F32)] * 7,
        compiler_params=_cparams(("arbitrary", "arbitrary")),
        name="rwkv_prep",
    )(*ins)


def _scan_kernel(r_ref, lw_ref, k_ref, v_ref, a_ref, b_ref, g_ref, lnw_ref, lnb_ref, rk_ref, y_ref,
                 *, n_iter, unroll):
    C = CHUNK
    ri = lax.broadcasted_iota(I32, (GRP, GRP), 0)
    ci = lax.broadcasted_iota(I32, (GRP, GRP), 1)
    same = (ri // C) == (ci // C)
    strict = same & ((ri % C) > (ci % C))
    incl = same & ((ri % C) >= (ci % C))
    eye = (ri == ci).astype(F32)
    ones2 = jnp.concatenate([same, same], axis=0).astype(BF16)
    tri = (lax.broadcasted_iota(I32, (C, C), 0) >= lax.broadcasted_iota(I32, (C, C), 1)).astype(BF16)
    tri3 = jnp.concatenate([tri, tri, tri], axis=1)
    lane_head = lax.broadcasted_iota(I32, (C, GRP), 1) // HEAD_DIM
    lnw, lnb, rk = lnw_ref[0], lnb_ref[0], rk_ref[0]

    def blockdiag(x):
        return jnp.concatenate([jnp.where(lane_head == hd, x, 0.0) for hd in range(GRP_HEADS)], axis=0)

    def chunk_terms(c_):
        sl = pl.ds(pl.multiple_of(c_ * C, C), C)
        r, lw, k, v = r_ref[0, 0, sl, :], lw_ref[0, 0, sl, :], k_ref[0, 0, sl, :], v_ref[0, 0, sl, :]
        a, b = a_ref[0, 0, sl, :], b_ref[0, 0, sl, :]
        cs = _dg(tri3, jnp.concatenate(_split3(lw), axis=0))
        p_in = jnp.exp(cs)
        p_inv = jnp.exp(-cs)
        a_d = blockdiag(a * jnp.exp(cs - lw))
        r_d = blockdiag(r * p_in)
        b_d = blockdiag(b * p_inv).astype(BF16)
        k_d = blockdiag(k * p_inv).astype(BF16)
        v_d = blockdiag(v).astype(BF16)
        m = _mm(jnp.concatenate([a_d, r_d], axis=0), jnp.concatenate([b_d, k_d], axis=0), NT)
        l_ab = jnp.where(strict, m[:GRP, :GRP], 0.0)
        l_ak = jnp.where(strict, m[:GRP, GRP:], 0.0)
        m_rb = jnp.where(incl, m[GRP:, :GRP], 0.0)
        m_rk = jnp.where(incl, m[GRP:, GRP:], 0.0)
        lv = _mm(jnp.concatenate([l_ak, m_rk], axis=0), v_d)
        t = eye + l_ab
        lp = _mm(l_ab, l_ab)
        step = 2
        while step < C:
            if 2 * step < C:
                x = _mm(lp, jnp.concatenate([t, lp], axis=1))
                t, lp = t + x[:, :GRP], x[:, GRP:]
            else:
                t = t + _mm(lp, t)
            step *= 2
        w = _mm(t, jnp.concatenate([a_d, lv[:GRP]], axis=1))
        rbw = _mm(m_rb, w)
        r2 = r_d + rbw[:, :GRP]
        o0 = rbw[:, GRP:] + lv[GRP:]
        tw = _mm(w, b_d, TN)
        plast = p_in[C - 1:C, :]
        gmat = (eye + tw[:GRP]) * plast
        hmat = (tw[GRP:] + _mm(v_d, k_d, TN)) * plast
        return r2, o0, gmat, hmat

    def body(it, s):
        terms = [chunk_terms(it * unroll + u) for u in range(unroll)]
        for u in range(unroll):
            r2, o0, gmat, hmat = terms[u]
            sl = pl.ds(pl.multiple_of((it * unroll + u) * C, C), C)
            od = _mm(r2, s, NT) + o0
            sh, slo = _split(s)
            gh, glo = _split(gmat)
            s = _dg(jnp.concatenate([sh, slo, sh], axis=1), jnp.concatenate([gh, gh, glo], axis=0)) + hmat
            o = od[:C]
            for hd in range(1, GRP_HEADS):
                o = o + od[hd * C:(hd + 1) * C]
            r, k, v = r_ref[0, 0, sl, :], k_ref[0, 0, sl, :], v_ref[0, 0, sl, :]
            xh, xl = _split(jnp.concatenate([o, r * k * rk], axis=0))
            sums = _dg(jnp.concatenate([xh, xl], axis=1), ones2)
            d = o - sums[:C] * (1.0 / HEAD_DIM)
            dh, dl = _split(d * d)
            var = _dg(jnp.concatenate([dh, dl], axis=1), ones2) * (1.0 / HEAD_DIM)
            on = d * lax.rsqrt(var + GN_EPS) * lnw + lnb
            y_ref[0, 0, sl, :] = ((on + sums[C:] * v) * g_ref[0, 0, sl, :]).astype(BF16)
        return s

    lax.fori_loop(0, n_iter, body, jnp.zeros((GRP, GRP), F32))


def _rwkv_scan(r, lw, k, v, a, b, g, lnw, lnb, rk):
    B, ngrp, Tp, _ = r.shape
    n_chunks = Tp // CHUNK
    unroll = SCAN_CHUNKS if n_chunks % SCAN_CHUNKS == 0 else 1
    seq = pl.BlockSpec((1, 1, Tp, GRP), lambda bi, j: (bi, j, 0, 0))
    par = pl.BlockSpec((1, 1, GRP), lambda bi, j: (j, 0, 0))
    return pl.pallas_call(
        functools.partial(_scan_kernel, n_iter=n_chunks // unroll, unroll=unroll),
        grid=(B, ngrp),
        in_specs=[seq] * 7 + [par] * 3,
        out_specs=seq,
        out_shape=jax.ShapeDtypeStruct((B, ngrp, Tp, GRP), BF16),
        compiler_params=_cparams(("arbitrary", "arbitrary")),
        name="rwkv_scan",
    )(r, lw, k, v, a, b, g, lnw, lnb, rk)


def _swap16(x):
    n = x.shape[-1]
    lane = lax.broadcasted_iota(I32, x.shape, x.ndim - 1)
    return jnp.where((lane % 32) < 16, pltpu.roll(x, n - 16, axis=x.ndim - 1), pltpu.roll(x, 16, axis=x.ndim - 1))


def _dsa_prep_kernel(pc_ref, cq_t, sq_t, ci_t, si_t, ck_t, sk_t, gq_ref, gkv_ref, wn_ref, wr_ref, wuk_ref,
                     wqi_ref, lnw_ref, lnb_ref, qcat_o, kcat_o, qidx_o, kidx_o, wh_o, *, scale):
    cq = _rms(pc_ref[0, :, :Q_RANK], gq_ref[...])
    ckv = _rms(pc_ref[0, :, Q_RANK:Q_RANK + KV_RANK], gkv_ref[...])
    slab = pc_ref[0, :, Q_RANK + KV_RANK:Q_RANK + KV_RANK + LANES]
    lane = lax.broadcasted_iota(I32, slab.shape, 1)
    in_idx = (lane >= ROPE) & (lane < ROPE + IDX_DIM)
    mean = jnp.sum(jnp.where(in_idx, slab, 0.0), axis=-1, keepdims=True) * (1.0 / IDX_DIM)
    dev = jnp.where(in_idx, slab - mean, 0.0)
    var = jnp.sum(dev * dev, axis=-1, keepdims=True) * (1.0 / IDX_DIM)
    kn = dev * lax.rsqrt(var + EPS) * lnw_ref[...] + lnb_ref[...]
    kslab = jnp.where(lane < ROPE, slab, kn)
    kslab = kslab * ck_t[...] + _swap16(kslab) * sk_t[...]
    tail = jnp.where(lane < ROPE, kslab, jnp.where(lane == LANES - 1, 1.0, 0.0))
    kcat_o[0] = jnp.concatenate([ckv, tail], axis=-1).astype(BF16)
    khi, klo = _split(kslab[:, ROPE:ROPE + IDX_DIM])
    kidx_o[0] = jnp.concatenate([khi, khi, klo], axis=-1)
    wh_o[0] = slab[:, ROPE + IDX_DIM:ROPE + IDX_DIM + IDX_HEADS] * (IDX_HEADS ** -0.5 * IDX_DIM ** -0.5)

    q_nope = _mm(cq, wn_ref[...])
    q_rope = _mm(cq, wr_ref[...])
    q_rope = q_rope * cq_t[...] + _swap16(q_rope) * sq_t[...]
    q_idx = _mm3(cq, wqi_ref[...])
    q_idx = q_idx * ci_t[...] + _swap16(q_idx) * si_t[...]
    for hd in range(DSA_HEADS):
        q_lat = _mm(q_nope[:, hd * NOPE:(hd + 1) * NOPE], wuk_ref[hd], NT)
        qc = jnp.concatenate([q_lat, q_rope[:, hd * ROPE:(hd + 1) * ROPE]], axis=-1) * scale
        qcat_o[0, hd] = qc.astype(BF16)
    for hd in range(IDX_HEADS):
        qhi, qlo = _split(q_idx[:, hd * IDX_DIM:(hd + 1) * IDX_DIM])
        qidx_o[0, hd] = jnp.concatenate([qhi, qlo, qhi], axis=-1)


def _dsa_prep(pc, tables, params, tb):
    B, Tp, ND = pc.shape
    row = lambda c: pl.BlockSpec((1, tb, c), lambda b, t: (b, t, 0))
    tab = lambda a: pl.BlockSpec((tb, a.shape[1]), lambda b, t: (t, 0))
    head = lambda n, c: pl.BlockSpec((1, n, tb, c), lambda b, t: (b, 0, t, 0))
    scale = float((NOPE + ROPE) ** -0.5 * math.log2(math.e))
    return pl.pallas_call(
        functools.partial(_dsa_prep_kernel, scale=scale),
        grid=(B, Tp // tb),
        in_specs=[row(ND)] + [tab(a) for a in tables] + [_const(p.shape) for p in params],
        out_specs=[head(DSA_HEADS, QCAT), row(KCAT), head(IDX_HEADS, IDX3), row(IDX3), row(IDX_HEADS)],
        out_shape=[jax.ShapeDtypeStruct((B, DSA_HEADS, Tp, QCAT), BF16),
                   jax.ShapeDtypeStruct((B, Tp, KCAT), BF16),
                   jax.ShapeDtypeStruct((B, IDX_HEADS, Tp, IDX3), BF16),
                   jax.ShapeDtypeStruct((B, Tp, IDX3), BF16),
                   jax.ShapeDtypeStruct((B, Tp, IDX_HEADS), F32)],
        compiler_params=_cparams(("arbitrary", "arbitrary")),
        name="dsa_prep",
    )(pc, *tables, *params)


def _dsa_kernel(qcat_ref, qidx_ref, wh_ref, kcat_ref, kidx_ref, wuv_ref, y_ref,
                keys_ref, mask_ref, mx_ref, acc_ref, *, topk, pos_bits):
    i = pl.program_id(1)
    nj = (i * QB + QB + KB - 1) // KB
    nsub = KB // LANES
    qpos = i * QB + lax.broadcasted_iota(I32, (KB, QB), 1)
    krow = lax.broadcasted_iota(I32, (KB, QB), 0)
    q_idx = qidx_ref[0].reshape(IDX_HEADS * QB, IDX3)
    w_head = wh_ref[0]

    def score_chunk(j, carry):
        kc = kidx_ref[0, pl.ds(pl.multiple_of(j * KB, KB), KB), :]
        s = jnp.maximum(_dg(kc, q_idx, NT), 0.0)
        tot = s[:, :QB] * w_head[0:1, :]
        for hd in range(1, IDX_HEADS):
            tot = tot + s[:, hd * QB:(hd + 1) * QB] * w_head[hd:hd + 1, :]
        tot = jnp.where(tot == 0.0, 0.0, tot)
        bits = pltpu.bitcast(tot, I32)
        key = jnp.where(bits < 0, bits ^ 0x7FFFFFFF, bits)
        keys_ref[j] = jnp.where(j * KB + krow <= qpos, key, INT_MIN)
        return carry

    lax.fori_loop(0, nj, score_chunk, 0)

    def count(pred):
        def step(j, acc):
            hit = pred(keys_ref[j], j).astype(I32)
            return acc + jnp.sum(hit.reshape(KB // 8, 8, QB), axis=0)
        acc = lax.fori_loop(0, nj, step, jnp.zeros((8, QB), I32))
        return jnp.sum(acc, axis=0, keepdims=True)

    v0 = jnp.where(count(lambda kk, j: kk >= 0) >= topk, 0, INT_MIN).astype(I32)

    def vbit(t, v):
        cand = v + lax.shift_left(jnp.int32(1), 30 - t)
        return jnp.where(count(lambda kk, j: kk >= cand) >= topk, cand, v)

    vth = lax.fori_loop(0, 31, vbit, v0)
    n_ge = count(lambda kk, j: kk >= vth)
    need = topk - count(lambda kk, j: kk > vth)
    excess_ties = jnp.max(((n_ge > topk) & (vth > INT_MIN)).astype(I32)) > 0

    def tie_break():
        def jbit(t, jv):
            cand = jv + lax.shift_left(jnp.int32(1), pos_bits - 1 - t)
            c = count(lambda kk, j: (kk == vth) & (j * KB + krow < cand))
            return jnp.where(c < need, cand, jv)
        return lax.fori_loop(0, pos_bits, jbit, jnp.zeros((1, QB), I32))

    jth = lax.cond(excess_ties, tie_break, lambda: jnp.full((1, QB), (1 << pos_bits) - 1, I32))

    def mask_chunk(j, carry):
        key = keys_ref[j]
        kpos = j * KB + krow
        sel = (key >= vth) & ((key > vth) | (kpos <= jth)) & (kpos <= qpos)
        mask_ref[j] = jnp.transpose(sel.astype(F32))
        return carry

    lax.fori_loop(0, nj, mask_chunk, 0)

    q_all = qcat_ref[0].reshape(DSA_HEADS * QB, QCAT)
    q_all = jnp.concatenate([q_all, jnp.zeros((DSA_HEADS * QB, KCAT - QCAT), BF16)], axis=1)
    hpg = DSA_HEADS // ATT_GROUPS
    rows = hpg * QB

    def masked_scores(g, kc, msk):
        s = _dg(q_all[g * rows:(g + 1) * rows], kc, NT).reshape(hpg, QB, KB)
        return jnp.where(msk[None], s, -jnp.inf)

    mx_ref[...] = jnp.full_like(mx_ref, -jnp.inf)

    def row_max(j, carry):
        kc = kcat_ref[0, pl.ds(pl.multiple_of(j * KB, KB), KB), :]
        msk = mask_ref[j] > 0.5
        for g in range(ATT_GROUPS):
            s = masked_scores(g, kc, msk)
            m = mx_ref[g * rows:(g + 1) * rows, :].reshape(hpg, QB, LANES)
            for c in range(nsub):
                m = jnp.maximum(m, s[:, :, c * LANES:(c + 1) * LANES])
            mx_ref[g * rows:(g + 1) * rows, :] = m.reshape(rows, LANES)
        return carry

    lax.fori_loop(0, nj, row_max, 0)
    mx_ref[...] = jnp.broadcast_to(jnp.max(mx_ref[...], axis=-1, keepdims=True), mx_ref.shape)

    acc_ref[...] = jnp.zeros_like(acc_ref)

    def attend(j, carry):
        kc = kcat_ref[0, pl.ds(pl.multiple_of(j * KB, KB), KB), :]
        msk = mask_ref[j] > 0.5
        for g in range(ATT_GROUPS):
            s = masked_scores(g, kc, msk)
            m = mx_ref[g * rows:(g + 1) * rows, :].reshape(hpg, QB, LANES)
            p = jnp.concatenate([jnp.exp2(s[:, :, c * LANES:(c + 1) * LANES] - m) for c in range(nsub)], axis=-1)
            acc_ref[g * rows:(g + 1) * rows, :] += _dg(p.astype(BF16).reshape(rows, KB), kc)
        return carry

    lax.fori_loop(0, nj, attend, 0)
    o_lat = acc_ref[:, :KV_RANK] / acc_ref[:, KCAT - 1:KCAT]
    y_ref[0] = jnp.concatenate([_mm(o_lat[hd * QB:(hd + 1) * QB], wuv_ref[hd]) for hd in range(DSA_HEADS)],
                               axis=-1).astype(BF16)


def _dsa(qcat, kcat, qidx, kidx, wh, wuv, topk):
    B, _, Tp, _ = qcat.shape
    Tk = kcat.shape[1]
    pos_bits = max(1, (Tk - 1).bit_length())
    qh = lambda n, c: pl.BlockSpec((1, n, QB, c), lambda b, i: (b, 0, i, 0))
    whole = lambda c: pl.BlockSpec((1, Tk, c), lambda b, i: (b, 0, 0))
    return pl.pallas_call(
        functools.partial(_dsa_kernel, topk=topk, pos_bits=pos_bits),
        grid=(B, Tp // QB),
        in_specs=[qh(DSA_HEADS, QCAT), qh(IDX_HEADS, IDX3), pl.BlockSpec((1, IDX_HEADS, QB), lambda b, i: (b, 0, i)),
                  whole(KCAT), whole(IDX3), _const(wuv.shape)],
        out_specs=pl.BlockSpec((1, QB, DSA_HEADS * V_DIM), lambda b, i: (b, i, 0)),
        out_shape=jax.ShapeDtypeStruct((B, Tp, DSA_HEADS * V_DIM), BF16),
        scratch_shapes=[pltpu.VMEM((Tk // KB, KB, QB), I32),
                        pltpu.VMEM((Tk // KB, QB, KB), F32),
                        pltpu.VMEM((DSA_HEADS * QB, LANES), F32),
                        pltpu.VMEM((DSA_HEADS * QB, KCAT), F32)],
        compiler_params=_cparams(("arbitrary", "arbitrary")),
        name="dsa_attn",
    )(qcat, qidx, wh, kcat, kidx, wuv)


def _merge_kernel(h_ref, gates_ref, ya_ref, yb_ref, yc_ref, pa_ref, pb_ref, pc_ref, wo_ref, o_ref, *, d):
    za = _dg(ya_ref[0], pa_ref[...])
    zb = _dg(yb_ref[0, 0], pb_ref[0:GRP, :])
    for j in range(1, yb_ref.shape[1]):
        zb = zb + _dg(yb_ref[0, j], pb_ref[j * GRP:(j + 1) * GRP, :])
    zc = _dg(yc_ref[0], pc_ref[...])
    merged = gates_ref[0, :, :d] * za + gates_ref[0, :, d:2 * d] * zb + gates_ref[0, :, 2 * d:] * zc
    o_ref[0] = h_ref[0] + _mm(merged, wo_ref[...])


def _merge(h, gates, ya, yb, yc, pa, pb, pc, wo, tb):
    B, Tp, D = h.shape
    row = lambda c: pl.BlockSpec((1, tb, c), lambda b, t: (b, t, 0))
    return pl.pallas_call(
        functools.partial(_merge_kernel, d=D),
        grid=(B, Tp // tb),
        in_specs=[row(D), row(gates.shape[2]), row(ya.shape[2]),
                  pl.BlockSpec((1, yb.shape[1], tb, GRP), lambda b, t: (b, 0, t, 0)), row(yc.shape[2])]
                 + [_const(w.shape) for w in (pa, pb, pc, wo)],
        out_specs=row(D),
        out_shape=jax.ShapeDtypeStruct((B, Tp, D), F32),
        compiler_params=_cparams(("arbitrary", "arbitrary")),
        name="merge",
    )(h, gates, ya, yb, yc, pa, pb, pc, wo)


def _first_index_of_max(x, valid, idx):
    big = jnp.int32(1 << 20)
    mx = jnp.max(jnp.where(valid, x, -jnp.inf), axis=-1, keepdims=True)
    first = jnp.min(jnp.where(valid & (x == mx), idx, big), axis=-1, keepdims=True)
    return mx, first


def _moe_kernel(h_ref, gn_ref, wg_ref, bg_ref, we_ref, be_ref, w13_ref, w2_ref, gf_ref, o_ref,
                hn_ref, comb_ref, *, hidden, final):
    e = pl.program_id(1)

    @pl.when(e == 0)
    def _():
        h = h_ref[...]
        hn = _rms(h, gn_ref[...])
        hn_ref[...] = hn.astype(BF16)
        gl = _mm3(hn, wg_ref[...]) + bg_ref[...]
        el = _mm3(hn, we_ref[...]) + be_ref[...]
        gidx = lax.broadcasted_iota(I32, gl.shape, 1)
        gmax, gsel = _first_index_of_max(gl, gidx >= 0, gidx)
        p_grp = 1.0 / jnp.sum(jnp.exp(gl - gmax), axis=-1, keepdims=True)
        eidx = lax.broadcasted_iota(I32, el.shape, 1)
        in_grp = (eidx // EPG) == gsel
        t1, i1 = _first_index_of_max(el, in_grp, eidx)
        t2, i2 = _first_index_of_max(el, in_grp & (eidx != i1), eidx)
        e2 = jnp.exp(t2 - t1)
        w1 = p_grp / (1.0 + e2)
        w2 = p_grp * e2 / (1.0 + e2)
        comb_ref[...] = jnp.where(eidx == i1, w1, 0.0) + jnp.where(eidx == i2, w2, 0.0)
        o_ref[...] = h

    hn = hn_ref[...]
    eidx = lax.broadcasted_iota(I32, comb_ref.shape, 1)
    ce = jnp.sum(jnp.where(eidx == e, comb_ref[...], 0.0), axis=-1, keepdims=True)
    x13 = _dg(hn, w13_ref[0])
    x1, x3 = x13[:, :hidden], x13[:, hidden:]
    hid = x1 * _sigmoid(x1) * x3 * ce
    o_ref[...] += _mm(hid, w2_ref[0])

    if final:
        @pl.when(e == pl.num_programs(1) - 1)
        def _():
            o_ref[...] = _rms(o_ref[...], gf_ref[...])


def _moe(h2, gn, wg, bg, we, be, w13, w2, gf, final, tb):
    n, D = h2.shape
    E, _, H2 = w13.shape
    row = pl.BlockSpec((tb, D), lambda r, e: (r, 0))
    return pl.pallas_call(
        functools.partial(_moe_kernel, hidden=H2 // 2, final=final),
        grid=(n // tb, E),
        in_specs=[row] + [_const(a.shape) for a in (gn, wg, bg, we, be)]
                 + [pl.BlockSpec((1, D, H2), lambda r, e: (e, 0, 0)),
                    pl.BlockSpec((1, H2 // 2, D), lambda r, e: (e, 0, 0)), _const(gf.shape)],
        out_specs=row,
        out_shape=jax.ShapeDtypeStruct((n, D), F32),
        scratch_shapes=[pltpu.VMEM((tb, D), BF16), pltpu.VMEM((tb, E), F32)],
        compiler_params=_cparams(("arbitrary", "arbitrary")),
        name="moe",
    )(h2, gn, wg, bg, we, be, w13, w2, gf)


def _rope_tables(tp):
    inv = ROPE_THETA ** (-jnp.arange(0, ROPE, 2, dtype=F32) / ROPE)
    ang = jnp.arange(tp, dtype=F32)[:, None] * inv[None, :]
    cos, sin = jnp.cos(ang), jnp.sin(ang)
    c32 = jnp.concatenate([cos, cos], axis=-1)
    s32 = jnp.concatenate([-sin, sin], axis=-1)
    one32, zero32 = jnp.ones_like(c32), jnp.zeros_like(c32)
    cq, sq = jnp.tile(c32, (1, DSA_HEADS)), jnp.tile(s32, (1, DSA_HEADS))
    ci = jnp.tile(jnp.concatenate([c32, one32], axis=-1), (1, IDX_HEADS))
    si = jnp.tile(jnp.concatenate([s32, zero32], axis=-1), (1, IDX_HEADS))
    ck = jnp.concatenate([c32, c32, one32, one32], axis=-1)
    sk = jnp.concatenate([s32, s32, zero32, zero32], axis=-1)
    return cq, sq, ci, si, ck, sk


def _pad_cols(w, n):
    return jnp.pad(w, ((0, 0), (0, n - w.shape[1])))


def kernel(x, meta_tokens, norm_mix, w_in, mu_shift, pool_w, pool_scale, rwkv_w0, rwkv_w2, rwkv_a0, rwkv_a2, rwkv_g2, rwkv_k_k, rwkv_k_a, rwkv_r_k, rwkv_ln_w, rwkv_ln_b, vres_w_down, vres_mu, vres_v0, vres_v2, mla_q_norm, mla_kv_norm, mla_w_uq, mla_w_uk, mla_w_uv, idx_w_q, idx_k_norm_w, idx_k_norm_b, w_proj_a, w_proj_b, w_proj_c, w_out, norm_ffn, router_w_group, router_b_group, router_w_expert, router_b_expert, expert_w1, expert_w3, expert_w2, norm_final):
    B, S, D = x.shape
    depth = w_in.shape[0]
    T = S + N_META
    Tp = -(-T // LANES) * LANES
    topk = min(MAX_TOPK, S // 4)
    c_a = pool_scale.shape[1]
    c_b = rwkv_w0.shape[1]
    c_shift = mu_shift.shape[1]
    npair = c_b // GRP
    o0 = 3 * D
    o1 = o0 + c_a
    o2 = o1 + c_shift
    c_dsa = Q_RANK + KV_RANK + ROPE + IDX_DIM + IDX_HEADS
    o3 = o2 + c_dsa
    nr = -(-(c_shift + VRES_RANK) // LANES) * LANES
    nd = -(-c_dsa // LANES) * LANES
    tb_in = _row_block(Tp, 288)
    tb_row = _row_block(Tp, 576)
    tb_moe = _row_block(B * Tp, 1152)
    row1 = lambda v: v.reshape(1, -1)

    h = jnp.concatenate([jnp.broadcast_to(meta_tokens[None].astype(x.dtype), (B, N_META, D)), x,
                         jnp.zeros((B, Tp - T, D), x.dtype)], axis=1)
    tables = _rope_tables(Tp)
    v_first = None
    for l in range(depth):
        w_l = w_in[l]
        w_r = w_l[:, o1:o2]
        mu = mu_shift[l]
        if l > 0:
            w_r = jnp.concatenate([w_r, vres_w_down[l - 1]], axis=1)
            mu = jnp.concatenate([mu, vres_mu[l - 1]])
        gates, ya, pb, pc = _in_proj(
            h, row1(norm_mix[l]), w_l[:, :o0].astype(BF16), w_l[:, o0:o1].astype(BF16),
            _pad_cols(w_r, nr).astype(BF16), _pad_cols(w_l[:, o2:o3], nd).astype(BF16),
            _pad_cols(row1(mu), nr), pool_w[l].astype(BF16), row1(pool_scale[l]), tb_in)

        prep_params = [row1(rwkv_w0[l]), rwkv_w2[l], row1(rwkv_a0[l]), rwkv_a2[l], rwkv_g2[l],
                       row1(rwkv_k_k[l]), row1(rwkv_k_a[l])]
        if l > 0:
            prep_params += [row1(vres_v0[l - 1]), vres_v2[l - 1]]
        r_, lw_, k_, v_, a_, b_, g_ = _rwkv_prep(pb, v_first if l > 0 else None, prep_params, c_b, tb_in)
        if l == 0:
            v_first = v_
        yb = _rwkv_scan(r_, lw_, k_, v_, a_, b_, g_, rwkv_ln_w[l].reshape(npair, 1, GRP),
                        rwkv_ln_b[l].reshape(npair, 1, GRP), rwkv_r_k[l].reshape(npair, 1, GRP))

        w_uq = mla_w_uq[l].reshape(Q_RANK, DSA_HEADS, NOPE + ROPE)
        idx_slab = lambda v: jnp.pad(v, (ROPE, LANES - ROPE - IDX_DIM)).reshape(1, LANES)
        dsa_params = [row1(mla_q_norm[l]), row1(mla_kv_norm[l]),
                      w_uq[:, :, :NOPE].reshape(Q_RANK, DSA_HEADS * NOPE).astype(BF16),
                      w_uq[:, :, NOPE:].reshape(Q_RANK, DSA_HEADS * ROPE).astype(BF16),
                      mla_w_uk[l].astype(BF16), idx_w_q[l], idx_slab(idx_k_norm_w[l]), idx_slab(idx_k_norm_b[l])]
        qcat, kcat, qidx, kidx, wh = _dsa_prep(pc, tables, dsa_params, tb_in)
        key_pad = ((0, 0), (0, -(-Tp // KB) * KB - Tp), (0, 0))
        yc = _dsa(qcat, jnp.pad(kcat, key_pad), qidx, jnp.pad(kidx, key_pad), jnp.swapaxes(wh, 1, 2),
                  mla_w_uv[l].astype(BF16), topk)

        h = _merge(h, gates, ya, yb, yc, w_proj_a[l].astype(BF16), w_proj_b[l].astype(BF16),
                   w_proj_c[l].astype(BF16), w_out[l].astype(BF16), tb_row)

        w13 = jnp.concatenate([expert_w1[l], expert_w3[l]], axis=-1).astype(BF16)
        h = _moe(h.reshape(B * Tp, D), row1(norm_ffn[l]), router_w_group[l], row1(router_b_group[l]),
                 router_w_expert[l], row1(router_b_expert[l]), w13, expert_w2[l].astype(BF16),
                 row1(norm_final), l == depth - 1, tb_moe).reshape(B, Tp, D)
    return h[:, N_META:T]
```

```python
import functools
import math

import jax
import jax.numpy as jnp
from jax import lax
from jax.experimental import pallas as pl
from jax.experimental.pallas import tpu as pltpu

F32 = jnp.float32
BF16 = jnp.bfloat16
I32 = jnp.int32

N_META = 16
EPS = 1e-6
ROPE_THETA = 10000.0
POOL_WINDOWS = (2, 4, 8, 16)
POOL_HALO = 16
HEAD_DIM = 64
GN_EPS = 64e-5
DECAY_RANK = 64
A_RANK = 64
GATE_RANK = 128
VRES_RANK = 32
DSA_HEADS = 16
NOPE = 64
ROPE = 32
V_DIM = 64
Q_RANK = 256
KV_RANK = 128
IDX_HEADS = 8
IDX_DIM = 64
MAX_TOPK = 256
N_GROUPS = 4
EPG = 4
N_EXPERTS = N_GROUPS * EPG
ROUTER_EXP0 = 8

LANES = 128
VMEM_LIMIT = 56 * 1024 * 1024

CHUNK = 64
GRP_HEADS = 4
GRP = GRP_HEADS * HEAD_DIM
SCAN_CHUNKS = 4
QB = 128
KB = 512
ATT_GROUPS = 4
QCAT = KV_RANK + ROPE
KCAT = 2 * LANES
IDX3 = 3 * IDX_DIM
INT_MIN = -2147483648
NEG_BIG = -1e30

NN = (((1,), (0,)), ((), ()))
NT = (((1,), (1,)), ((), ()))
TN = (((0,), (0,)), ((), ()))


def _dg(a, b, dims=NN):
    return lax.dot_general(a, b, dims, preferred_element_type=F32)


def _mm(a, b, dims=NN):
    return _dg(a.astype(BF16), b.astype(BF16), dims)


def _split(x):
    hi = x.astype(BF16)
    lo = (x - hi.astype(F32)).astype(BF16)
    return hi, lo


def _split3(x):
    a1 = x.astype(BF16)
    r1 = x - a1.astype(F32)
    a2 = r1.astype(BF16)
    a3 = (r1 - a2.astype(F32)).astype(BF16)
    return a1, a2, a3


def _mm3(a, b, dims=NN):
    ah, al = _split(a)
    bh, bl = _split(b)
    return _dg(ah, bh, dims) + (_dg(al, bh, dims) + _dg(ah, bl, dims))


def _mm_rhs_exact(a, m, dims=NN):
    ah, al = _split(a)
    return _dg(ah, m, dims) + _dg(al, m, dims)


def _cparams(sem):
    return pltpu.CompilerParams(dimension_semantics=sem, vmem_limit_bytes=VMEM_LIMIT)


def _row_block(n, target, mult=16):
    best = None
    for d in range(mult, min(n, target) + 1, mult):
        if n % d == 0:
            best = d
    assert best is not None, n
    return best


def _const(shape):
    nd = len(shape)
    return pl.BlockSpec(shape, lambda *_: (0,) * nd, pipeline_mode=pl.Buffered(1))


def _sigmoid(x):
    return 1.0 / (1.0 + jnp.exp(-x))


def _softplus(x):
    return jnp.maximum(x, 0.0) + jnp.log(1.0 + jnp.exp(-jnp.abs(x)))


def _rms(x, g):
    return x * lax.rsqrt(jnp.mean(x * x, axis=-1, keepdims=True) + EPS) * g


def _in_kernel(h_ref, g_ref, wg_ref, wp_ref, wr_ref, wd_ref, mu_ref, poolw_ref, pscale_ref,
               gates_ref, ya_ref, pb_ref, pc_ref, carry_pool, carry_r, *, tb):
    t = pl.program_id(1)

    @pl.when(t == 0)
    def _():
        carry_pool[...] = jnp.zeros_like(carry_pool)
        carry_r[...] = jnp.zeros_like(carry_r)

    u = _rms(h_ref[0], g_ref[...]).astype(BF16)

    gates_ref[0] = _sigmoid(_dg(u, wg_ref[...]))
    pc_ref[0] = _dg(u, wd_ref[...])

    p_r = _dg(u, wr_ref[...])
    row = lax.broadcasted_iota(I32, (tb, 1), 0)
    prev = jnp.where(row == 0, carry_r[7:8, :], pltpu.roll(p_r, 1, axis=0))
    pb_ref[0] = p_r + (prev - p_r) * mu_ref[...]
    carry_r[...] = p_r[tb - 8:, :]

    p_pool = _dg(u, wp_ref[...])
    ext = jnp.concatenate([carry_pool[...], p_pool], axis=0)
    carry_pool[...] = p_pool[tb - POOL_HALO:, :]
    tpos = (t * tb + row).astype(F32)
    gdim = p_pool.shape[1] // len(POOL_WINDOWS)
    ys = []
    for g, w in enumerate(POOL_WINDOWS):
        s, step = ext[:, g * gdim:(g + 1) * gdim], 1
        while step < w:
            s = s + pltpu.roll(s, step, axis=0)
            step *= 2
        cnt = jnp.minimum(tpos + 1.0, float(w))
        pooled = s[POOL_HALO:, :] / cnt - p_pool[:, g * gdim:(g + 1) * gdim]
        ys.append(_mm(pooled, poolw_ref[g]))
    ya_ref[0] = (jnp.concatenate(ys, axis=-1) * pscale_ref[...]).astype(BF16)


def _in_proj(h, g, wg, wp, wr, wd, mu, poolw, pscale, tb):
    B, Tp, D = h.shape
    blk = lambda c: pl.BlockSpec((1, tb, c), lambda b, t: (b, t, 0))
    widths = (wg.shape[1], wp.shape[1], wr.shape[1], wd.shape[1])
    return pl.pallas_call(
        functools.partial(_in_kernel, tb=tb),
        grid=(B, Tp // tb),
        in_specs=[blk(D)] + [_const(a.shape) for a in (g, wg, wp, wr, wd, mu, poolw, pscale)],
        out_specs=[blk(c) for c in widths],
        out_shape=[jax.ShapeDtypeStruct((B, Tp, widths[0]), F32),
                   jax.ShapeDtypeStruct((B, Tp, widths[1]), BF16),
                   jax.ShapeDtypeStruct((B, Tp, widths[2]), F32),
                   jax.ShapeDtypeStruct((B, Tp, widths[3]), F32)],
        scratch_shapes=[pltpu.VMEM((POOL_HALO, widths[1]), F32), pltpu.VMEM((8, widths[2]), F32)],
        compiler_params=_cparams(("arbitrary", "arbitrary")),
        name="in_proj",
    )(h, g, wg, wp, wr, wd, mu, poolw, pscale)


def _rwkv_prep_kernel(*refs, c, use_vres):
    if use_vres:
        (pb_ref, vf_ref, w0_ref, w2_ref, a0_ref, a2_ref, g2_ref, kk_ref, ka_ref, v0_ref, v2_ref,
         r_o, lw_o, k_o, v_o, a_o, b_o, g_o) = refs
    else:
        (pb_ref, w0_ref, w2_ref, a0_ref, a2_ref, g2_ref, kk_ref, ka_ref,
         r_o, lw_o, k_o, v_o, a_o, b_o, g_o) = refs
    o = 3 * c
    wd = pb_ref[0, :, o:o + DECAY_RANK]
    ad = pb_ref[0, :, o + DECAY_RANK:o + DECAY_RANK + A_RANK]
    gd = pb_ref[0, :, o + DECAY_RANK + A_RANK:o + DECAY_RANK + A_RANK + GATE_RANK]
    w_log = -_softplus(-(w0_ref[...] + _mm3(jnp.tanh(wd), w2_ref[...]))) - 0.5
    lw = -jnp.exp(w_log)
    a = _sigmoid(a0_ref[...] + _mm3(ad, a2_ref[...]))
    g = _mm3(_sigmoid(gd), g2_ref[...])
    if use_vres:
        vd = pb_ref[0, :, o + DECAY_RANK + A_RANK + GATE_RANK:o + DECAY_RANK + A_RANK + GATE_RANK + VRES_RANK]
        vmix = _sigmoid(v0_ref[...] + _mm3(vd, v2_ref[...]))
    ri = lax.broadcasted_iota(I32, (GRP, GRP), 0) // HEAD_DIM
    ci = lax.broadcasted_iota(I32, (GRP, GRP), 1) // HEAD_DIM
    ones_blk = (ri == ci).astype(BF16)
    for j in range(c // GRP):
        sl = slice(j * GRP, (j + 1) * GRP)
        r = pb_ref[0, :, j * GRP:(j + 1) * GRP]
        k = pb_ref[0, :, c + j * GRP:c + (j + 1) * GRP]
        v = pb_ref[0, :, 2 * c + j * GRP:2 * c + (j + 1) * GRP]
        if use_vres:
            v = v + (vf_ref[0, j] - v) * vmix[:, sl]
        aj = a[:, sl]
        kk = k * kk_ref[:, sl]
        nrm = jnp.sqrt(_mm_rhs_exact(kk * kk, ones_blk))
        kk = kk / jnp.maximum(nrm, 1e-12)
        r_o[0, j] = r
        lw_o[0, j] = lw[:, sl]
        k_o[0, j] = k * (1.0 + (aj - 1.0) * ka_ref[:, sl])
        v_o[0, j] = v
        a_o[0, j] = -kk
        b_o[0, j] = kk * aj
        g_o[0, j] = g[:, sl]


def _rwkv_prep(pb, v_first, params, c, tb):
    B, Tp, NR = pb.shape
    npair = c // GRP
    use_vres = v_first is not None
    pair_blk = pl.BlockSpec((1, npair, tb, GRP), lambda b, t: (b, 0, t, 0))
    ins = [pb] + ([v_first] if use_vres else []) + list(params)
    in_specs = ([pl.BlockSpec((1, tb, NR), lambda b, t: (b, t, 0))] + ([pair_blk] if use_vres else [])
                + [_const(p.shape) for p in params])
    return pl.pallas_call(
        functools.partial(_rwkv_prep_kernel, c=c, use_vres=use_vres),
        grid=(B, Tp // tb),
        in_specs=in_specs,
        out_specs=[pair_blk] * 7,
        out_shape=[jax.ShapeDtypeStruct((B, npair, Tp, GRP), F32)] * 7,
        compiler_params=_cparams(("arbitrary", "arbitrary")),
        name="rwkv_prep",
    )(*ins)


def _scan_kernel(r_ref, lw_ref, k_ref, v_ref, a_ref, b_ref, g_ref, lnw_ref, lnb_ref, rk_ref, y_ref,
                 *, n_chunks, unroll):
    C = CHUNK

    def rows(c_):
        return pl.ds(c_ * C, C) if isinstance(c_, int) else pl.ds(pl.multiple_of(c_ * C, C), C)

    ri = lax.broadcasted_iota(I32, (GRP, GRP), 0)
    ci = lax.broadcasted_iota(I32, (GRP, GRP), 1)
    same = (ri // C) == (ci // C)
    strict = same & ((ri % C) > (ci % C))
    incl = same & ((ri % C) >= (ci % C))
    eye = (ri == ci).astype(F32)
    ones2 = jnp.concatenate([same, same], axis=0).astype(BF16)
    tri = (lax.broadcasted_iota(I32, (C, C), 0) >= lax.broadcasted_iota(I32, (C, C), 1)).astype(BF16)
    tri3 = jnp.concatenate([tri, tri, tri], axis=1)
    lane_head = lax.broadcasted_iota(I32, (C, GRP), 1) // HEAD_DIM
    lnw, lnb, rk = lnw_ref[0], lnb_ref[0], rk_ref[0]

    def blockdiag(x):
        return jnp.concatenate([jnp.where(lane_head == hd, x, 0.0) for hd in range(GRP_HEADS)], axis=0)

    def each(f, *lists):
        return [f(*xs) for xs in zip(*lists)]

    def chunk_terms(chunk_ids):
        sls = [rows(c_) for c_ in chunk_ids]
        lw = [lw_ref[0, 0, sl, :] for sl in sls]
        cs = each(lambda x: _dg(tri3, jnp.concatenate(_split3(x), axis=0)), lw)
        p_in = each(jnp.exp, cs)
        p_inv = each(lambda x: jnp.exp(-x), cs)
        a_d = each(lambda sl, c, l: blockdiag(a_ref[0, 0, sl, :] * jnp.exp(c - l)), sls, cs, lw)
        r_d = each(lambda sl, p: blockdiag(r_ref[0, 0, sl, :] * p), sls, p_in)
        b_d = each(lambda sl, p: blockdiag(b_ref[0, 0, sl, :] * p).astype(BF16), sls, p_inv)
        k_d = each(lambda sl, p: blockdiag(k_ref[0, 0, sl, :] * p).astype(BF16), sls, p_inv)
        v_d = each(lambda sl: blockdiag(v_ref[0, 0, sl, :]).astype(BF16), sls)
        m = each(lambda a, r, b, k: _mm(jnp.concatenate([a, r], axis=0), jnp.concatenate([b, k], axis=0), NT),
                 a_d, r_d, b_d, k_d)
        l_ab = each(lambda x: jnp.where(strict, x[:GRP, :GRP], 0.0), m)
        lk = each(lambda x: jnp.concatenate([jnp.where(strict, x[:GRP, GRP:], 0.0),
                                             jnp.where(incl, x[GRP:, GRP:], 0.0)], axis=0), m)
        m_rb = each(lambda x: jnp.where(incl, x[GRP:, :GRP], 0.0), m)
        lv = each(_mm, lk, v_d)
        t = each(lambda x: eye + x, l_ab)
        lp = each(lambda x: _mm(x, x), l_ab)
        step = 2
        while step < C:
            if 2 * step < C:
                x = each(lambda p, q: _mm(p, jnp.concatenate([q, p], axis=1)), lp, t)
                t = each(lambda q, y: q + y[:, :GRP], t, x)
                lp = each(lambda y: y[:, GRP:], x)
            else:
                t = each(lambda q, p: q + _mm(p, q), t, lp)
            step *= 2
        w = each(lambda q, a, y: _mm(q, jnp.concatenate([a, y[:GRP]], axis=1)), t, a_d, lv)
        rbw = each(_mm, m_rb, w)
        r2 = each(lambda r, y: r + y[:, :GRP], r_d, rbw)
        o0 = each(lambda y, z: y[:, GRP:] + z[GRP:], rbw, lv)
        tw = each(lambda x, b: _mm(x, b, TN), w, b_d)
        vk = each(lambda v, k: _mm(v, k, TN), v_d, k_d)
        gmat = each(lambda x, p: (eye + x[:GRP]) * p[C - 1:C, :], tw, p_in)
        hmat = each(lambda x, y, p: (x[GRP:] + y) * p[C - 1:C, :], tw, vk, p_in)
        return list(zip(r2, o0, gmat, hmat))

    def cat_split(x):
        hi, lo = _split(x)
        return jnp.concatenate([hi, lo], axis=1)

    def advance(chunk_ids, s):
        sls = [rows(c_) for c_ in chunk_ids]
        ods = []
        for r2, o0, gmat, hmat in chunk_terms(chunk_ids):
            ods.append(_mm(r2, s, NT) + o0)
            sh, slo = _split(s)
            gh, glo = _split(gmat)
            s = _dg(jnp.concatenate([sh, slo, sh], axis=1), jnp.concatenate([gh, gh, glo], axis=0)) + hmat
        o = each(lambda od: sum(od[hd * C:(hd + 1) * C] for hd in range(1, GRP_HEADS)) + od[:C], ods)
        sums = each(lambda sl, o_: _dg(cat_split(jnp.concatenate(
            [o_, r_ref[0, 0, sl, :] * k_ref[0, 0, sl, :] * rk], axis=0)), ones2), sls, o)
        d = each(lambda o_, z: o_ - z[:C] * (1.0 / HEAD_DIM), o, sums)
        var = each(lambda x: _dg(cat_split(x * x), ones2) * (1.0 / HEAD_DIM), d)
        for sl, d_, var_, z in zip(sls, d, var, sums):
            on = d_ * lax.rsqrt(var_ + GN_EPS) * lnw + lnb
            y_ref[0, 0, sl, :] = ((on + z[C:] * v_ref[0, 0, sl, :]) * g_ref[0, 0, sl, :]).astype(BF16)
        return s

    n_main = n_chunks // unroll
    s = lax.fori_loop(0, n_main, lambda it, s_: advance([it * unroll + u for u in range(unroll)], s_),
                      jnp.zeros((GRP, GRP), F32))
    if n_chunks % unroll:
        advance(list(range(n_main * unroll, n_chunks)), s)


def _rwkv_scan(r, lw, k, v, a, b, g, lnw, lnb, rk):
    B, ngrp, Tp, _ = r.shape
    n_chunks = Tp // CHUNK
    unroll = min(SCAN_CHUNKS, n_chunks)
    seq = pl.BlockSpec((1, 1, Tp, GRP), lambda bi, j: (bi, j, 0, 0))
    par = pl.BlockSpec((1, 1, GRP), lambda bi, j: (j, 0, 0))
    return pl.pallas_call(
        functools.partial(_scan_kernel, n_chunks=n_chunks, unroll=unroll),
        grid=(B, ngrp),
        in_specs=[seq] * 7 + [par] * 3,
        out_specs=seq,
        out_shape=jax.ShapeDtypeStruct((B, ngrp, Tp, GRP), BF16),
        compiler_params=_cparams(("arbitrary", "arbitrary")),
        name="rwkv_scan",
    )(r, lw, k, v, a, b, g, lnw, lnb, rk)


def _swap16(x):
    n = x.shape[-1]
    lane = lax.broadcasted_iota(I32, x.shape, x.ndim - 1)
    return jnp.where((lane % 32) < 16, pltpu.roll(x, n - 16, axis=x.ndim - 1), pltpu.roll(x, 16, axis=x.ndim - 1))


def _dsa_prep_kernel(pc_ref, cq_t, sq_t, ci_t, si_t, ck_t, sk_t, gq_ref, gkv_ref, wn_ref, wr_ref, wuk_ref,
                     wqi_ref, lnw_ref, lnb_ref, qcat_o, kcat_o, qidx_o, kidx_o, wh_o, *, scale):
    cq = _rms(pc_ref[0, :, :Q_RANK], gq_ref[...])
    ckv = _rms(pc_ref[0, :, Q_RANK:Q_RANK + KV_RANK], gkv_ref[...])
    slab = pc_ref[0, :, Q_RANK + KV_RANK:Q_RANK + KV_RANK + LANES]
    lane = lax.broadcasted_iota(I32, slab.shape, 1)
    in_idx = (lane >= ROPE) & (lane < ROPE + IDX_DIM)
    mean = jnp.sum(jnp.where(in_idx, slab, 0.0), axis=-1, keepdims=True) * (1.0 / IDX_DIM)
    dev = jnp.where(in_idx, slab - mean, 0.0)
    var = jnp.sum(dev * dev, axis=-1, keepdims=True) * (1.0 / IDX_DIM)
    kn = dev * lax.rsqrt(var + EPS) * lnw_ref[...] + lnb_ref[...]
    kslab = jnp.where(lane < ROPE, slab, kn)
    kslab = kslab * ck_t[...] + _swap16(kslab) * sk_t[...]
    tail = jnp.where(lane < ROPE, kslab, 0.0)
    kcat_o[0] = jnp.concatenate([ckv, tail], axis=-1).astype(BF16)
    khi, klo = _split(kslab[:, ROPE:ROPE + IDX_DIM])
    kidx_o[0] = jnp.concatenate([khi, khi, klo], axis=-1)
    wh_o[0] = slab[:, ROPE + IDX_DIM:ROPE + IDX_DIM + IDX_HEADS] * (IDX_HEADS ** -0.5 * IDX_DIM ** -0.5)

    q_nope = _mm(cq, wn_ref[...])
    q_rope = _mm(cq, wr_ref[...])
    q_rope = q_rope * cq_t[...] + _swap16(q_rope) * sq_t[...]
    q_idx = _mm3(cq, wqi_ref[...])
    q_idx = q_idx * ci_t[...] + _swap16(q_idx) * si_t[...]
    for hd in range(DSA_HEADS):
        q_lat = _mm(q_nope[:, hd * NOPE:(hd + 1) * NOPE], wuk_ref[hd], NT)
        qc = jnp.concatenate([q_lat, q_rope[:, hd * ROPE:(hd + 1) * ROPE]], axis=-1) * scale
        qcat_o[0, hd] = qc.astype(BF16)
    for hd in range(IDX_HEADS):
        qhi, qlo = _split(q_idx[:, hd * IDX_DIM:(hd + 1) * IDX_DIM])
        qidx_o[0, hd] = jnp.concatenate([qhi, qlo, qhi], axis=-1)


def _dsa_prep(pc, tables, params, tb):
    B, Tp, ND = pc.shape
    row = lambda c: pl.BlockSpec((1, tb, c), lambda b, t: (b, t, 0))
    tab = lambda a: pl.BlockSpec((tb, a.shape[1]), lambda b, t: (t, 0))
    head = lambda n, c: pl.BlockSpec((1, n, tb, c), lambda b, t: (b, 0, t, 0))
    scale = float((NOPE + ROPE) ** -0.5 * math.log2(math.e))
    return pl.pallas_call(
        functools.partial(_dsa_prep_kernel, scale=scale),
        grid=(B, Tp // tb),
        in_specs=[row(ND)] + [tab(a) for a in tables] + [_const(p.shape) for p in params],
        out_specs=[head(DSA_HEADS, QCAT), row(KCAT), head(IDX_HEADS, IDX3), row(IDX3), row(IDX_HEADS)],
        out_shape=[jax.ShapeDtypeStruct((B, DSA_HEADS, Tp, QCAT), BF16),
                   jax.ShapeDtypeStruct((B, Tp, KCAT), BF16),
                   jax.ShapeDtypeStruct((B, IDX_HEADS, Tp, IDX3), BF16),
                   jax.ShapeDtypeStruct((B, Tp, IDX3), BF16),
                   jax.ShapeDtypeStruct((B, Tp, IDX_HEADS), F32)],
        compiler_params=_cparams(("arbitrary", "arbitrary")),
        name="dsa_prep",
    )(pc, *tables, *params)


def _dsa_kernel(qcat_ref, qidx_ref, wh_ref, kcat_ref, kidx_ref, wuv_ref, y_ref,
                keys_ref, mask_ref, mx_ref, acc_ref, *, topk, pos_bits):
    i = pl.program_id(1)
    nj = (i * QB + QB + KB - 1) // KB
    nsub = KB // LANES
    qpos = i * QB + lax.broadcasted_iota(I32, (KB, QB), 1)
    krow = lax.broadcasted_iota(I32, (KB, QB), 0)
    q_idx = qidx_ref[0].reshape(IDX_HEADS * QB, IDX3)
    w_head = wh_ref[0]

    def score_chunk(j, carry):
        kc = kidx_ref[0, pl.ds(pl.multiple_of(j * KB, KB), KB), :]
        s = jnp.maximum(_dg(kc, q_idx, NT), 0.0)
        tot = s[:, :QB] * w_head[0:1, :]
        for hd in range(1, IDX_HEADS):
            tot = tot + s[:, hd * QB:(hd + 1) * QB] * w_head[hd:hd + 1, :]
        tot = jnp.where(tot == 0.0, 0.0, tot)
        bits = pltpu.bitcast(tot, I32)
        key = jnp.where(bits < 0, bits ^ 0x7FFFFFFF, bits)
        keys_ref[j] = jnp.where(j * KB + krow <= qpos, key, INT_MIN)
        return carry

    lax.fori_loop(0, nj, score_chunk, 0)

    def count(pred):
        def step(j, acc):
            hit = pred(keys_ref[j], j).astype(I32)
            return acc + jnp.sum(hit.reshape(KB // 8, 8, QB), axis=0)
        acc = lax.fori_loop(0, nj, step, jnp.zeros((8, QB), I32))
        return jnp.sum(acc, axis=0, keepdims=True)

    v0 = jnp.where(count(lambda kk, j: kk >= 0) >= topk, 0, INT_MIN).astype(I32)

    def vbit(t, v):
        cand = v + lax.shift_left(jnp.int32(1), 30 - t)
        return jnp.where(count(lambda kk, j: kk >= cand) >= topk, cand, v)

    vth = lax.fori_loop(0, 31, vbit, v0)
    n_ge = count(lambda kk, j: kk >= vth)
    need = topk - count(lambda kk, j: kk > vth)
    excess_ties = jnp.max(((n_ge > topk) & (vth > INT_MIN)).astype(I32)) > 0

    def tie_break():
        def jbit(t, jv):
            cand = jv + lax.shift_left(jnp.int32(1), pos_bits - 1 - t)
            c = count(lambda kk, j: (kk == vth) & (j * KB + krow < cand))
            return jnp.where(c < need, cand, jv)
        return lax.fori_loop(0, pos_bits, jbit, jnp.zeros((1, QB), I32))

    jth = lax.cond(excess_ties, tie_break, lambda: jnp.full((1, QB), (1 << pos_bits) - 1, I32))

    def mask_chunk(j, carry):
        key = keys_ref[j]
        kpos = j * KB + krow
        sel = (key >= vth) & ((key > vth) | (kpos <= jth)) & (kpos <= qpos)
        mask_ref[j] = jnp.transpose(sel.astype(F32))
        return carry

    lax.fori_loop(0, nj, mask_chunk, 0)

    q_all = qcat_ref[0].reshape(DSA_HEADS * QB, QCAT)
    q_all = jnp.concatenate([q_all, jnp.zeros((DSA_HEADS * QB, KCAT - QCAT), BF16)], axis=1)
    hpg = DSA_HEADS // ATT_GROUPS
    rows = hpg * QB

    def masked_scores(g, kc, msk):
        s = _dg(q_all[g * rows:(g + 1) * rows], kc, NT).reshape(hpg, QB, KB)
        return jnp.where(msk[None], s, -jnp.inf)

    mx_ref[...] = jnp.full_like(mx_ref, -jnp.inf)

    def row_max(j, carry):
        kc = kcat_ref[0, pl.ds(pl.multiple_of(j * KB, KB), KB), :]
        msk = mask_ref[j] > 0.5
        for g in range(ATT_GROUPS):
            s = masked_scores(g, kc, msk)
            m = mx_ref[g * rows:(g + 1) * rows, :].reshape(hpg, QB, LANES)
            for c in range(nsub):
                m = jnp.maximum(m, s[:, :, c * LANES:(c + 1) * LANES])
            mx_ref[g * rows:(g + 1) * rows, :] = m.reshape(rows, LANES)
        return carry

    lax.fori_loop(0, nj, row_max, 0)
    mx_ref[...] = jnp.broadcast_to(jnp.max(mx_ref[...], axis=-1, keepdims=True), mx_ref.shape)

    acc_ref[...] = jnp.zeros_like(acc_ref)

    def attend(j, carry):
        kc = kcat_ref[0, pl.ds(pl.multiple_of(j * KB, KB), KB), :]
        val = jnp.concatenate([kc[:, :KV_RANK], jnp.ones((KB, KCAT - KV_RANK), BF16)], axis=1)
        msk = mask_ref[j] > 0.5
        for g in range(ATT_GROUPS):
            s = masked_scores(g, kc, msk)
            m = mx_ref[g * rows:(g + 1) * rows, :].reshape(hpg, QB, LANES)
            p = jnp.concatenate([jnp.exp2(s[:, :, c * LANES:(c + 1) * LANES] - m) for c in range(nsub)], axis=-1)
            acc_ref[g * rows:(g + 1) * rows, :] += _dg(p.astype(BF16).reshape(rows, KB), val)
        return carry

    lax.fori_loop(0, nj, attend, 0)
    o_lat = acc_ref[:, :KV_RANK] / acc_ref[:, KV_RANK:]
    y_ref[0] = jnp.concatenate([_mm(o_lat[hd * QB:(hd + 1) * QB], wuv_ref[hd]) for hd in range(DSA_HEADS)],
                               axis=-1).astype(BF16)


def _dsa(qcat, kcat, qidx, kidx, wh, wuv, topk):
    B, _, Tp, _ = qcat.shape
    Tk = kcat.shape[1]
    pos_bits = max(1, (Tk - 1).bit_length())
    qh = lambda n, c: pl.BlockSpec((1, n, QB, c), lambda b, i: (b, 0, i, 0))
    whole = lambda c: pl.BlockSpec((1, Tk, c), lambda b, i: (b, 0, 0))
    return pl.pallas_call(
        functools.partial(_dsa_kernel, topk=topk, pos_bits=pos_bits),
        grid=(B, Tp // QB),
        in_specs=[qh(DSA_HEADS, QCAT), qh(IDX_HEADS, IDX3), pl.BlockSpec((1, IDX_HEADS, QB), lambda b, i: (b, 0, i)),
                  whole(KCAT), whole(IDX3), _const(wuv.shape)],
        out_specs=pl.BlockSpec((1, QB, DSA_HEADS * V_DIM), lambda b, i: (b, i, 0)),
        out_shape=jax.ShapeDtypeStruct((B, Tp, DSA_HEADS * V_DIM), BF16),
        scratch_shapes=[pltpu.VMEM((Tk // KB, KB, QB), I32),
                        pltpu.VMEM((Tk // KB, QB, KB), F32),
                        pltpu.VMEM((DSA_HEADS * QB, LANES), F32),
                        pltpu.VMEM((DSA_HEADS * QB, KCAT), F32)],
        compiler_params=_cparams(("arbitrary", "arbitrary")),
        name="dsa_attn",
    )(qcat, qidx, wh, kcat, kidx, wuv)


def _merge_kernel(h_ref, gates_ref, ya_ref, yb_ref, yc_ref, pa_ref, pb_ref, pc_ref, wo_ref, o_ref, *, d):
    za = _dg(ya_ref[0], pa_ref[...])
    zb = _dg(yb_ref[0, 0], pb_ref[0:GRP, :])
    for j in range(1, yb_ref.shape[1]):
        zb = zb + _dg(yb_ref[0, j], pb_ref[j * GRP:(j + 1) * GRP, :])
    zc = _dg(yc_ref[0], pc_ref[...])
    merged = gates_ref[0, :, :d] * za + gates_ref[0, :, d:2 * d] * zb + gates_ref[0, :, 2 * d:] * zc
    o_ref[0] = h_ref[0] + _mm(merged, wo_ref[...])


def _merge(h, gates, ya, yb, yc, pa, pb, pc, wo, tb):
    B, Tp, D = h.shape
    row = lambda c: pl.BlockSpec((1, tb, c), lambda b, t: (b, t, 0))
    return pl.pallas_call(
        functools.partial(_merge_kernel, d=D),
        grid=(B, Tp // tb),
        in_specs=[row(D), row(gates.shape[2]), row(ya.shape[2]),
                  pl.BlockSpec((1, yb.shape[1], tb, GRP), lambda b, t: (b, 0, t, 0)), row(yc.shape[2])]
                 + [_const(w.shape) for w in (pa, pb, pc, wo)],
        out_specs=row(D),
        out_shape=jax.ShapeDtypeStruct((B, Tp, D), F32),
        compiler_params=_cparams(("arbitrary", "arbitrary")),
        name="merge",
    )(h, gates, ya, yb, yc, pa, pb, pc, wo)


def _first_index_of_max(x, valid, idx):
    big = jnp.int32(1 << 20)
    mx = jnp.max(jnp.where(valid, x, -jnp.inf), axis=0, keepdims=True)
    first = jnp.min(jnp.where(valid & (x == mx), idx, big), axis=0, keepdims=True)
    return mx, first


def _moe_kernel(h_ref, gn_ref, wr_ref, br_ref, w13_ref, w2_ref, gf_ref, o_ref,
                hn_ref, comb_ref, *, hidden, final):
    e = pl.program_id(1)

    @pl.when(e == 0)
    def _():
        h = h_ref[...]
        hn = _rms(h, gn_ref[...])
        hn_ref[...] = hn.astype(BF16)
        lt = jnp.transpose(_mm3(hn, wr_ref[...]) + br_ref[...])
        gidx = lax.broadcasted_iota(I32, (ROUTER_EXP0, lt.shape[1]), 0)
        is_grp = gidx < N_GROUPS
        gl = lt[:ROUTER_EXP0]
        gmax, gsel = _first_index_of_max(gl, is_grp, gidx)
        p_grp = 1.0 / jnp.sum(jnp.where(is_grp, jnp.exp(gl - gmax), 0.0), axis=0, keepdims=True)
        el = lt[ROUTER_EXP0:ROUTER_EXP0 + N_EXPERTS]
        eidx = lax.broadcasted_iota(I32, el.shape, 0)
        in_grp = (eidx // EPG) == gsel
        t1, i1 = _first_index_of_max(el, in_grp, eidx)
        t2, i2 = _first_index_of_max(el, in_grp & (eidx != i1), eidx)
        e2 = jnp.exp(t2 - t1)
        w1 = p_grp / (1.0 + e2)
        w2 = p_grp * e2 / (1.0 + e2)
        comb = jnp.where(eidx == i1, w1, 0.0) + jnp.where(eidx == i2, w2, 0.0)
        comb_ref[...] = jnp.transpose(
            jnp.concatenate([comb, jnp.zeros((LANES - N_EXPERTS, comb.shape[1]), F32)], axis=0))
        o_ref[...] = h

    hn = hn_ref[...]
    lane = lax.broadcasted_iota(I32, comb_ref.shape, 1)
    ce = jnp.sum(jnp.where(lane == e, comb_ref[...], 0.0), axis=-1, keepdims=True)
    x13 = _dg(hn, w13_ref[0])
    x1, x3 = x13[:, :hidden], x13[:, hidden:]
    hid = x1 * _sigmoid(x1) * x3 * ce
    o_ref[...] += _mm(hid, w2_ref[0])

    if final:
        @pl.when(e == pl.num_programs(1) - 1)
        def _():
            o_ref[...] = _rms(o_ref[...], gf_ref[...])


def _moe(h2, gn, wr, br, w13, w2, gf, final, tb):
    n, D = h2.shape
    E, _, H2 = w13.shape
    row = pl.BlockSpec((tb, D), lambda r, e: (r, 0))
    return pl.pallas_call(
        functools.partial(_moe_kernel, hidden=H2 // 2, final=final),
        grid=(n // tb, E),
        in_specs=[row] + [_const(a.shape) for a in (gn, wr, br)]
                 + [pl.BlockSpec((1, D, H2), lambda r, e: (e, 0, 0)),
                    pl.BlockSpec((1, H2 // 2, D), lambda r, e: (e, 0, 0)), _const(gf.shape)],
        out_specs=row,
        out_shape=jax.ShapeDtypeStruct((n, D), F32),
        scratch_shapes=[pltpu.VMEM((tb, D), BF16), pltpu.VMEM((tb, LANES), F32)],
        compiler_params=_cparams(("arbitrary", "arbitrary")),
        name="moe",
    )(h2, gn, wr, br, w13, w2, gf)


def _rope_tables(tp):
    inv = ROPE_THETA ** (-jnp.arange(0, ROPE, 2, dtype=F32) / ROPE)
    ang = jnp.arange(tp, dtype=F32)[:, None] * inv[None, :]
    cos, sin = jnp.cos(ang), jnp.sin(ang)
    c32 = jnp.concatenate([cos, cos], axis=-1)
    s32 = jnp.concatenate([-sin, sin], axis=-1)
    one32, zero32 = jnp.ones_like(c32), jnp.zeros_like(c32)
    cq, sq = jnp.tile(c32, (1, DSA_HEADS)), jnp.tile(s32, (1, DSA_HEADS))
    ci = jnp.tile(jnp.concatenate([c32, one32], axis=-1), (1, IDX_HEADS))
    si = jnp.tile(jnp.concatenate([s32, zero32], axis=-1), (1, IDX_HEADS))
    ck = jnp.concatenate([c32, c32, one32, one32], axis=-1)
    sk = jnp.concatenate([s32, s32, zero32, zero32], axis=-1)
    return cq, sq, ci, si, ck, sk


def _pad_cols(w, n):
    return jnp.pad(w, ((0, 0), (0, n - w.shape[1])))


def kernel(x, meta_tokens, norm_mix, w_in, mu_shift, pool_w, pool_scale, rwkv_w0, rwkv_w2, rwkv_a0, rwkv_a2, rwkv_g2, rwkv_k_k, rwkv_k_a, rwkv_r_k, rwkv_ln_w, rwkv_ln_b, vres_w_down, vres_mu, vres_v0, vres_v2, mla_q_norm, mla_kv_norm, mla_w_uq, mla_w_uk, mla_w_uv, idx_w_q, idx_k_norm_w, idx_k_norm_b, w_proj_a, w_proj_b, w_proj_c, w_out, norm_ffn, router_w_group, router_b_group, router_w_expert, router_b_expert, expert_w1, expert_w3, expert_w2, norm_final):
    B, S, D = x.shape
    depth = w_in.shape[0]
    T = S + N_META
    Tp = -(-T // LANES) * LANES
    topk = min(MAX_TOPK, S // 4)
    c_a = pool_scale.shape[1]
    c_b = rwkv_w0.shape[1]
    c_shift = mu_shift.shape[1]
    npair = c_b // GRP
    o0 = 3 * D
    o1 = o0 + c_a
    o2 = o1 + c_shift
    c_dsa = Q_RANK + KV_RANK + ROPE + IDX_DIM + IDX_HEADS
    o3 = o2 + c_dsa
    nr = -(-(c_shift + VRES_RANK) // LANES) * LANES
    nd = -(-c_dsa // LANES) * LANES
    tb_in = _row_block(Tp, 576)
    tb_row = _row_block(Tp, 576)
    tb_moe = _row_block(B * Tp, 1152, LANES)
    row1 = lambda v: v.reshape(1, -1)

    h = jnp.concatenate([jnp.broadcast_to(meta_tokens[None].astype(x.dtype), (B, N_META, D)), x,
                         jnp.zeros((B, Tp - T, D), x.dtype)], axis=1)
    tables = _rope_tables(Tp)
    v_first = None
    for l in range(depth):
        w_l = w_in[l]
        w_r = w_l[:, o1:o2]
        mu = mu_shift[l]
        if l > 0:
            w_r = jnp.concatenate([w_r, vres_w_down[l - 1]], axis=1)
            mu = jnp.concatenate([mu, vres_mu[l - 1]])
        gates, ya, pb, pc = _in_proj(
            h, row1(norm_mix[l]), w_l[:, :o0].astype(BF16), w_l[:, o0:o1].astype(BF16),
            _pad_cols(w_r, nr).astype(BF16), _pad_cols(w_l[:, o2:o3], nd).astype(BF16),
            _pad_cols(row1(mu), nr), pool_w[l].astype(BF16), row1(pool_scale[l]), tb_in)

        prep_params = [row1(rwkv_w0[l]), rwkv_w2[l], row1(rwkv_a0[l]), rwkv_a2[l], rwkv_g2[l],
                       row1(rwkv_k_k[l]), row1(rwkv_k_a[l])]
        if l > 0:
            prep_params += [row1(vres_v0[l - 1]), vres_v2[l - 1]]
        r_, lw_, k_, v_, a_, b_, g_ = _rwkv_prep(pb, v_first if l > 0 else None, prep_params, c_b, tb_in)
        if l == 0:
            v_first = v_
        yb = _rwkv_scan(r_, lw_, k_, v_, a_, b_, g_, rwkv_ln_w[l].reshape(npair, 1, GRP),
                        rwkv_ln_b[l].reshape(npair, 1, GRP), rwkv_r_k[l].reshape(npair, 1, GRP))

        w_uq = mla_w_uq[l].reshape(Q_RANK, DSA_HEADS, NOPE + ROPE)
        idx_slab = lambda v: jnp.pad(v, (ROPE, LANES - ROPE - IDX_DIM)).reshape(1, LANES)
        dsa_params = [row1(mla_q_norm[l]), row1(mla_kv_norm[l]),
                      w_uq[:, :, :NOPE].reshape(Q_RANK, DSA_HEADS * NOPE).astype(BF16),
                      w_uq[:, :, NOPE:].reshape(Q_RANK, DSA_HEADS * ROPE).astype(BF16),
                      mla_w_uk[l].astype(BF16), idx_w_q[l], idx_slab(idx_k_norm_w[l]), idx_slab(idx_k_norm_b[l])]
        qcat, kcat, qidx, kidx, wh = _dsa_prep(pc, tables, dsa_params, tb_in)
        key_pad = ((0, 0), (0, -(-Tp // KB) * KB - Tp), (0, 0))
        yc = _dsa(qcat, jnp.pad(kcat, key_pad), qidx, jnp.pad(kidx, key_pad), jnp.swapaxes(wh, 1, 2),
                  mla_w_uv[l].astype(BF16), topk)

        h = _merge(h, gates, ya, yb, yc, w_proj_a[l].astype(BF16), w_proj_b[l].astype(BF16),
                   w_proj_c[l].astype(BF16), w_out[l].astype(BF16), tb_row)

        w13 = jnp.concatenate([expert_w1[l], expert_w3[l]], axis=-1).astype(BF16)
        gpad = ROUTER_EXP0 - N_GROUPS
        w_router = _pad_cols(jnp.concatenate([jnp.pad(router_w_group[l], ((0, 0), (0, gpad))),
                                              router_w_expert[l]], axis=1), LANES)
        b_router = _pad_cols(row1(jnp.concatenate([jnp.pad(router_b_group[l], (0, gpad)), router_b_expert[l]])),
                             LANES)
        h = _moe(h.reshape(B * Tp, D), row1(norm_ffn[l]), w_router, b_router, w13, expert_w2[l].astype(BF16),
                 row1(norm_final), l == depth - 1, tb_moe).reshape(B, Tp, D)
    return h[:, N_META:T]
```

```python
import functools
import math

import jax
import jax.numpy as jnp
from jax import lax
from jax.experimental import pallas as pl
from jax.experimental.pallas import tpu as pltpu

F32 = jnp.float32
BF16 = jnp.bfloat16
I32 = jnp.int32

N_META = 16
EPS = 1e-6
ROPE_THETA = 10000.0
POOL_WINDOWS = (2, 4, 8, 16)
POOL_HALO = 16
HEAD_DIM = 64
GN_EPS = 64e-5
DECAY_RANK = 64
A_RANK = 64
GATE_RANK = 128
VRES_RANK = 32
DSA_HEADS = 16
NOPE = 64
ROPE = 32
V_DIM = 64
Q_RANK = 256
KV_RANK = 128
IDX_HEADS = 8
IDX_DIM = 64
MAX_TOPK = 256
N_GROUPS = 4
EPG = 4
N_EXPERTS = N_GROUPS * EPG
ROUTER_EXP0 = 8

LANES = 128
VMEM_LIMIT = 56 * 1024 * 1024

CHUNK = 64
GRP_HEADS = 4
GRP = GRP_HEADS * HEAD_DIM
SCAN_CHUNKS = 4
QB = 128
KB = 512
ATT_GROUPS = 4
QCAT = KV_RANK + ROPE
KCAT = 2 * LANES
IDX3 = 3 * IDX_DIM
INT_MIN = -2147483648
SHIFT_SLACK = 100.0

NN = (((1,), (0,)), ((), ()))
NT = (((1,), (1,)), ((), ()))
TN = (((0,), (0,)), ((), ()))


def _dg(a, b, dims=NN):
    return lax.dot_general(a, b, dims, preferred_element_type=F32)


def _mm(a, b, dims=NN):
    return _dg(a.astype(BF16), b.astype(BF16), dims)


def _split(x):
    hi = x.astype(BF16)
    lo = (x - hi.astype(F32)).astype(BF16)
    return hi, lo


def _split3(x):
    a1 = x.astype(BF16)
    r1 = x - a1.astype(F32)
    a2 = r1.astype(BF16)
    a3 = (r1 - a2.astype(F32)).astype(BF16)
    return a1, a2, a3


def _mm3(a, b, dims=NN):
    ah, al = _split(a)
    bh, bl = _split(b)
    return _dg(ah, bh, dims) + (_dg(al, bh, dims) + _dg(ah, bl, dims))


def _mm_rhs_exact(a, m, dims=NN):
    ah, al = _split(a)
    return _dg(ah, m, dims) + _dg(al, m, dims)


def _cparams(sem):
    return pltpu.CompilerParams(dimension_semantics=sem, vmem_limit_bytes=VMEM_LIMIT)


def _row_block(n, target, mult=16):
    best = None
    for d in range(mult, min(n, target) + 1, mult):
        if n % d == 0:
            best = d
    assert best is not None, n
    return best


def _const(shape):
    nd = len(shape)
    return pl.BlockSpec(shape, lambda *_: (0,) * nd, pipeline_mode=pl.Buffered(1))


def _sigmoid(x):
    return 1.0 / (1.0 + jnp.exp(-x))


def _softplus(x):
    return jnp.maximum(x, 0.0) + jnp.log(1.0 + jnp.exp(-jnp.abs(x)))


def _rms(x, g):
    return x * lax.rsqrt(jnp.mean(x * x, axis=-1, keepdims=True) + EPS) * g


def _in_kernel(h_ref, g_ref, wg_ref, wp_ref, wr_ref, wd_ref, mu_ref, poolw_ref, pscale_ref,
               gates_ref, ya_ref, pb_ref, pc_ref, carry_pool, carry_r, *, tb):
    t = pl.program_id(1)

    @pl.when(t == 0)
    def _():
        carry_pool[...] = jnp.zeros_like(carry_pool)
        carry_r[...] = jnp.zeros_like(carry_r)

    u = _rms(h_ref[0], g_ref[...]).astype(BF16)

    gates_ref[0] = _sigmoid(_dg(u, wg_ref[...]))
    pc_ref[0] = _dg(u, wd_ref[...])

    p_r = _dg(u, wr_ref[...])
    row = lax.broadcasted_iota(I32, (tb, 1), 0)
    prev = jnp.where(row == 0, carry_r[7:8, :], pltpu.roll(p_r, 1, axis=0))
    pb_ref[0] = p_r + (prev - p_r) * mu_ref[...]
    carry_r[...] = p_r[tb - 8:, :]

    p_pool = _dg(u, wp_ref[...])
    ext = jnp.concatenate([carry_pool[...], p_pool], axis=0)
    carry_pool[...] = p_pool[tb - POOL_HALO:, :]
    tpos = (t * tb + row).astype(F32)
    gdim = p_pool.shape[1] // len(POOL_WINDOWS)
    ys = []
    for g, w in enumerate(POOL_WINDOWS):
        s, step = ext[:, g * gdim:(g + 1) * gdim], 1
        while step < w:
            s = s + pltpu.roll(s, step, axis=0)
            step *= 2
        cnt = jnp.minimum(tpos + 1.0, float(w))
        pooled = s[POOL_HALO:, :] / cnt - p_pool[:, g * gdim:(g + 1) * gdim]
        ys.append(_mm(pooled, poolw_ref[g]))
    ya_ref[0] = (jnp.concatenate(ys, axis=-1) * pscale_ref[...]).astype(BF16)


def _in_proj(h, g, wg, wp, wr, wd, mu, poolw, pscale, tb):
    B, Tp, D = h.shape
    blk = lambda c: pl.BlockSpec((1, tb, c), lambda b, t: (b, t, 0))
    widths = (wg.shape[1], wp.shape[1], wr.shape[1], wd.shape[1])
    return pl.pallas_call(
        functools.partial(_in_kernel, tb=tb),
        grid=(B, Tp // tb),
        in_specs=[blk(D)] + [_const(a.shape) for a in (g, wg, wp, wr, wd, mu, poolw, pscale)],
        out_specs=[blk(c) for c in widths],
        out_shape=[jax.ShapeDtypeStruct((B, Tp, widths[0]), F32),
                   jax.ShapeDtypeStruct((B, Tp, widths[1]), BF16),
                   jax.ShapeDtypeStruct((B, Tp, widths[2]), F32),
                   jax.ShapeDtypeStruct((B, Tp, widths[3]), F32)],
        scratch_shapes=[pltpu.VMEM((POOL_HALO, widths[1]), F32), pltpu.VMEM((8, widths[2]), F32)],
        compiler_params=_cparams(("arbitrary", "arbitrary")),
        name="in_proj",
    )(h, g, wg, wp, wr, wd, mu, poolw, pscale)


def _rwkv_prep_kernel(*refs, c, use_vres):
    if use_vres:
        (pb_ref, vf_ref, w0_ref, w2_ref, a0_ref, a2_ref, g2_ref, kk_ref, ka_ref, v0_ref, v2_ref,
         r_o, lw_o, k_o, v_o, a_o, b_o, g_o) = refs
    else:
        (pb_ref, w0_ref, w2_ref, a0_ref, a2_ref, g2_ref, kk_ref, ka_ref,
         r_o, lw_o, k_o, v_o, a_o, b_o, g_o) = refs
    o = 3 * c
    wd = pb_ref[0, :, o:o + DECAY_RANK]
    ad = pb_ref[0, :, o + DECAY_RANK:o + DECAY_RANK + A_RANK]
    gd = pb_ref[0, :, o + DECAY_RANK + A_RANK:o + DECAY_RANK + A_RANK + GATE_RANK]
    w_log = -_softplus(-(w0_ref[...] + _mm3(jnp.tanh(wd), w2_ref[...]))) - 0.5
    lw = -jnp.exp(w_log)
    a = _sigmoid(a0_ref[...] + _mm3(ad, a2_ref[...]))
    g = _mm3(_sigmoid(gd), g2_ref[...])
    if use_vres:
        vd = pb_ref[0, :, o + DECAY_RANK + A_RANK + GATE_RANK:o + DECAY_RANK + A_RANK + GATE_RANK + VRES_RANK]
        vmix = _sigmoid(v0_ref[...] + _mm3(vd, v2_ref[...]))
    ri = lax.broadcasted_iota(I32, (GRP, GRP), 0) // HEAD_DIM
    ci = lax.broadcasted_iota(I32, (GRP, GRP), 1) // HEAD_DIM
    ones_blk = (ri == ci).astype(BF16)
    for j in range(c // GRP):
        sl = slice(j * GRP, (j + 1) * GRP)
        r = pb_ref[0, :, j * GRP:(j + 1) * GRP]
        k = pb_ref[0, :, c + j * GRP:c + (j + 1) * GRP]
        v = pb_ref[0, :, 2 * c + j * GRP:2 * c + (j + 1) * GRP]
        if use_vres:
            v = v + (vf_ref[0, j] - v) * vmix[:, sl]
        aj = a[:, sl]
        kk = k * kk_ref[:, sl]
        nrm = jnp.sqrt(_mm_rhs_exact(kk * kk, ones_blk))
        kk = kk / jnp.maximum(nrm, 1e-12)
        r_o[0, j] = r
        lw_o[0, j] = lw[:, sl]
        k_o[0, j] = k * (1.0 + (aj - 1.0) * ka_ref[:, sl])
        v_o[0, j] = v
        a_o[0, j] = -kk
        b_o[0, j] = kk * aj
        g_o[0, j] = g[:, sl]


def _rwkv_prep(pb, v_first, params, c, tb):
    B, Tp, NR = pb.shape
    npair = c // GRP
    use_vres = v_first is not None
    pair_blk = pl.BlockSpec((1, npair, tb, GRP), lambda b, t: (b, 0, t, 0))
    ins = [pb] + ([v_first] if use_vres else []) + list(params)
    in_specs = ([pl.BlockSpec((1, tb, NR), lambda b, t: (b, t, 0))] + ([pair_blk] if use_vres else [])
                + [_const(p.shape) for p in params])
    return pl.pallas_call(
        functools.partial(_rwkv_prep_kernel, c=c, use_vres=use_vres),
        grid=(B, Tp // tb),
        in_specs=in_specs,
        out_specs=[pair_blk] * 7,
        out_shape=[jax.ShapeDtypeStruct((B, npair, Tp, GRP), F32)] * 7,
        compiler_params=_cparams(("arbitrary", "arbitrary")),
        name="rwkv_prep",
    )(*ins)


def _scan_kernel(r_ref, lw_ref, k_ref, v_ref, a_ref, b_ref, g_ref, lnw_ref, lnb_ref, rk_ref, y_ref,
                 *, n_chunks, unroll):
    C = CHUNK

    def rows(c_):
        return pl.ds(c_ * C, C) if isinstance(c_, int) else pl.ds(pl.multiple_of(c_ * C, C), C)

    ri = lax.broadcasted_iota(I32, (GRP, GRP), 0)
    ci = lax.broadcasted_iota(I32, (GRP, GRP), 1)
    same = (ri // C) == (ci // C)
    strict = same & ((ri % C) > (ci % C))
    incl = same & ((ri % C) >= (ci % C))
    eye = (ri == ci).astype(F32)
    ones2 = jnp.concatenate([same, same], axis=0).astype(BF16)
    tri = (lax.broadcasted_iota(I32, (C, C), 0) >= lax.broadcasted_iota(I32, (C, C), 1)).astype(BF16)
    tri3 = jnp.concatenate([tri, tri, tri], axis=1)
    lane_head = lax.broadcasted_iota(I32, (C, GRP), 1) // HEAD_DIM
    lnw, lnb, rk = lnw_ref[0], lnb_ref[0], rk_ref[0]

    def blockdiag(x):
        return jnp.concatenate([jnp.where(lane_head == hd, x, 0.0) for hd in range(GRP_HEADS)], axis=0)

    def each(f, *lists):
        return [f(*xs) for xs in zip(*lists)]

    def chunk_terms(chunk_ids):
        sls = [rows(c_) for c_ in chunk_ids]
        lw = [lw_ref[0, 0, sl, :] for sl in sls]
        cs = each(lambda x: _dg(tri3, jnp.concatenate(_split3(x), axis=0)), lw)
        p_in = each(jnp.exp, cs)
        p_inv = each(lambda x: jnp.exp(-x), cs)
        a_d = each(lambda sl, c, l: blockdiag(a_ref[0, 0, sl, :] * jnp.exp(c - l)), sls, cs, lw)
        r_d = each(lambda sl, p: blockdiag(r_ref[0, 0, sl, :] * p), sls, p_in)
        b_d = each(lambda sl, p: blockdiag(b_ref[0, 0, sl, :] * p).astype(BF16), sls, p_inv)
        k_d = each(lambda sl, p: blockdiag(k_ref[0, 0, sl, :] * p).astype(BF16), sls, p_inv)
        v_d = each(lambda sl: blockdiag(v_ref[0, 0, sl, :]).astype(BF16), sls)
        m = each(lambda a, r, b, k: _mm(jnp.concatenate([a, r], axis=0), jnp.concatenate([b, k], axis=0), NT),
                 a_d, r_d, b_d, k_d)
        l_ab = each(lambda x: jnp.where(strict, x[:GRP, :GRP], 0.0), m)
        lk = each(lambda x: jnp.concatenate([jnp.where(strict, x[:GRP, GRP:], 0.0),
                                             jnp.where(incl, x[GRP:, GRP:], 0.0)], axis=0), m)
        m_rb = each(lambda x: jnp.where(incl, x[GRP:, :GRP], 0.0), m)
        lv = each(_mm, lk, v_d)
        t = each(lambda x: eye + x, l_ab)
        lp = each(lambda x: _mm(x, x), l_ab)
        step = 2
        while step < C:
            if 2 * step < C:
                x = each(lambda p, q: _mm(p, jnp.concatenate([q, p], axis=1)), lp, t)
                t = each(lambda q, y: q + y[:, :GRP], t, x)
                lp = each(lambda y: y[:, GRP:], x)
            else:
                t = each(lambda q, p: q + _mm(p, q), t, lp)
            step *= 2
        w = each(lambda q, a, y: _mm(q, jnp.concatenate([a, y[:GRP]], axis=1)), t, a_d, lv)
        rbw = each(_mm, m_rb, w)
        r2 = each(lambda r, y: r + y[:, :GRP], r_d, rbw)
        o0 = each(lambda y, z: y[:, GRP:] + z[GRP:], rbw, lv)
        tw = each(lambda x, b: _mm(x, b, TN), w, b_d)
        vk = each(lambda v, k: _mm(v, k, TN), v_d, k_d)
        gmat = each(lambda x, p: (eye + x[:GRP]) * p[C - 1:C, :], tw, p_in)
        hmat = each(lambda x, y, p: (x[GRP:] + y) * p[C - 1:C, :], tw, vk, p_in)
        return list(zip(r2, o0, gmat, hmat))

    def cat_split(x):
        hi, lo = _split(x)
        return jnp.concatenate([hi, lo], axis=1)

    def advance(chunk_ids, s):
        sls = [rows(c_) for c_ in chunk_ids]
        ods = []
        for r2, o0, gmat, hmat in chunk_terms(chunk_ids):
            ods.append(_mm(r2, s, NT) + o0)
            sh, slo = _split(s)
            gh, glo = _split(gmat)
            s = _dg(jnp.concatenate([sh, slo, sh], axis=1), jnp.concatenate([gh, gh, glo], axis=0)) + hmat
        o = each(lambda od: sum(od[hd * C:(hd + 1) * C] for hd in range(1, GRP_HEADS)) + od[:C], ods)
        sums = each(lambda sl, o_: _dg(cat_split(jnp.concatenate(
            [o_, r_ref[0, 0, sl, :] * k_ref[0, 0, sl, :] * rk], axis=0)), ones2), sls, o)
        d = each(lambda o_, z: o_ - z[:C] * (1.0 / HEAD_DIM), o, sums)
        var = each(lambda x: _dg(cat_split(x * x), ones2) * (1.0 / HEAD_DIM), d)
        for sl, d_, var_, z in zip(sls, d, var, sums):
            on = d_ * lax.rsqrt(var_ + GN_EPS) * lnw + lnb
            y_ref[0, 0, sl, :] = ((on + z[C:] * v_ref[0, 0, sl, :]) * g_ref[0, 0, sl, :]).astype(BF16)
        return s

    n_main = n_chunks // unroll
    s = lax.fori_loop(0, n_main, lambda it, s_: advance([it * unroll + u for u in range(unroll)], s_),
                      jnp.zeros((GRP, GRP), F32))
    if n_chunks % unroll:
        advance(list(range(n_main * unroll, n_chunks)), s)


def _rwkv_scan(r, lw, k, v, a, b, g, lnw, lnb, rk):
    B, ngrp, Tp, _ = r.shape
    n_chunks = Tp // CHUNK
    unroll = min(SCAN_CHUNKS, n_chunks)
    seq = pl.BlockSpec((1, 1, Tp, GRP), lambda bi, j: (bi, j, 0, 0))
    par = pl.BlockSpec((1, 1, GRP), lambda bi, j: (j, 0, 0))
    return pl.pallas_call(
        functools.partial(_scan_kernel, n_chunks=n_chunks, unroll=unroll),
        grid=(B, ngrp),
        in_specs=[seq] * 7 + [par] * 3,
        out_specs=seq,
        out_shape=jax.ShapeDtypeStruct((B, ngrp, Tp, GRP), BF16),
        compiler_params=_cparams(("arbitrary", "arbitrary")),
        name="rwkv_scan",
    )(r, lw, k, v, a, b, g, lnw, lnb, rk)


def _swap16(x):
    n = x.shape[-1]
    lane = lax.broadcasted_iota(I32, x.shape, x.ndim - 1)
    return jnp.where((lane % 32) < 16, pltpu.roll(x, n - 16, axis=x.ndim - 1), pltpu.roll(x, 16, axis=x.ndim - 1))


def _dsa_prep_kernel(pc_ref, cq_t, sq_t, ci_t, si_t, ck_t, sk_t, gq_ref, gkv_ref, wn_ref, wr_ref, wuk_ref,
                     wqi_ref, lnw_ref, lnb_ref, qcat_o, kcat_o, qidx_o, kidx_o, wh_o, *, scale):
    cq = _rms(pc_ref[0, :, :Q_RANK], gq_ref[...])
    ckv = _rms(pc_ref[0, :, Q_RANK:Q_RANK + KV_RANK], gkv_ref[...])
    slab = pc_ref[0, :, Q_RANK + KV_RANK:Q_RANK + KV_RANK + LANES]
    lane = lax.broadcasted_iota(I32, slab.shape, 1)
    in_idx = (lane >= ROPE) & (lane < ROPE + IDX_DIM)
    mean = jnp.sum(jnp.where(in_idx, slab, 0.0), axis=-1, keepdims=True) * (1.0 / IDX_DIM)
    dev = jnp.where(in_idx, slab - mean, 0.0)
    var = jnp.sum(dev * dev, axis=-1, keepdims=True) * (1.0 / IDX_DIM)
    kn = dev * lax.rsqrt(var + EPS) * lnw_ref[...] + lnb_ref[...]
    kslab = jnp.where(lane < ROPE, slab, kn)
    kslab = kslab * ck_t[...] + _swap16(kslab) * sk_t[...]
    tail = jnp.where(lane < ROPE, kslab, 0.0)
    kcat_o[0] = jnp.concatenate([ckv, tail], axis=-1).astype(BF16)
    khi, klo = _split(kslab[:, ROPE:ROPE + IDX_DIM])
    kidx_o[0] = jnp.concatenate([khi, khi, klo], axis=-1)
    wh_o[0] = slab[:, ROPE + IDX_DIM:ROPE + IDX_DIM + IDX_HEADS] * (IDX_HEADS ** -0.5 * IDX_DIM ** -0.5)

    q_nope = _mm(cq, wn_ref[...])
    q_rope = _mm(cq, wr_ref[...])
    q_rope = q_rope * cq_t[...] + _swap16(q_rope) * sq_t[...]
    q_idx = _mm3(cq, wqi_ref[...])
    q_idx = q_idx * ci_t[...] + _swap16(q_idx) * si_t[...]
    for hd in range(DSA_HEADS):
        q_lat = _mm(q_nope[:, hd * NOPE:(hd + 1) * NOPE], wuk_ref[hd], NT)
        qc = jnp.concatenate([q_lat, q_rope[:, hd * ROPE:(hd + 1) * ROPE]], axis=-1) * scale
        qcat_o[0, hd] = qc.astype(BF16)
    for hd in range(IDX_HEADS):
        qhi, qlo = _split(q_idx[:, hd * IDX_DIM:(hd + 1) * IDX_DIM])
        qidx_o[0, hd] = jnp.concatenate([qhi, qlo, qhi], axis=-1)


def _dsa_prep(pc, tables, params, tb):
    B, Tp, ND = pc.shape
    row = lambda c: pl.BlockSpec((1, tb, c), lambda b, t: (b, t, 0))
    tab = lambda a: pl.BlockSpec((tb, a.shape[1]), lambda b, t: (t, 0))
    head = lambda n, c: pl.BlockSpec((1, n, tb, c), lambda b, t: (b, 0, t, 0))
    scale = float((NOPE + ROPE) ** -0.5 * math.log2(math.e))
    return pl.pallas_call(
        functools.partial(_dsa_prep_kernel, scale=scale),
        grid=(B, Tp // tb),
        in_specs=[row(ND)] + [tab(a) for a in tables] + [_const(p.shape) for p in params],
        out_specs=[head(DSA_HEADS, QCAT), row(KCAT), head(IDX_HEADS, IDX3), row(IDX3), row(IDX_HEADS)],
        out_shape=[jax.ShapeDtypeStruct((B, DSA_HEADS, Tp, QCAT), BF16),
                   jax.ShapeDtypeStruct((B, Tp, KCAT), BF16),
                   jax.ShapeDtypeStruct((B, IDX_HEADS, Tp, IDX3), BF16),
                   jax.ShapeDtypeStruct((B, Tp, IDX3), BF16),
                   jax.ShapeDtypeStruct((B, Tp, IDX_HEADS), F32)],
        compiler_params=_cparams(("arbitrary", "arbitrary")),
        name="dsa_prep",
    )(pc, *tables, *params)


def _dsa_kernel(qcat_ref, qidx_ref, wh_ref, kcat_ref, kidx_ref, wuv_ref, y_ref,
                keys_ref, mask_ref, mx_ref, top_ref, acc_ref, *, topk, pos_bits):
    i = pl.program_id(1)
    nj = (i * QB + QB + KB - 1) // KB
    nsub = KB // LANES
    qpos = i * QB + lax.broadcasted_iota(I32, (KB, QB), 1)
    krow = lax.broadcasted_iota(I32, (KB, QB), 0)
    q_idx = qidx_ref[0].reshape(IDX_HEADS * QB, IDX3)
    w_head = wh_ref[0]

    def score_chunk(j, carry):
        kc = kidx_ref[0, pl.ds(pl.multiple_of(j * KB, KB), KB), :]
        s = jnp.maximum(_dg(kc, q_idx, NT), 0.0)
        tot = s[:, :QB] * w_head[0:1, :]
        for hd in range(1, IDX_HEADS):
            tot = tot + s[:, hd * QB:(hd + 1) * QB] * w_head[hd:hd + 1, :]
        tot = jnp.where(tot == 0.0, 0.0, tot)
        bits = pltpu.bitcast(tot, I32)
        key = jnp.where(bits < 0, bits ^ 0x7FFFFFFF, bits)
        keys_ref[j] = jnp.where(j * KB + krow <= qpos, key, INT_MIN)
        return carry

    lax.fori_loop(0, nj, score_chunk, 0)

    def count(pred):
        def step(j, acc):
            hit = pred(keys_ref[j], j).astype(I32)
            return acc + jnp.sum(hit.reshape(KB // 8, 8, QB), axis=0)
        acc = lax.fori_loop(0, nj, step, jnp.zeros((8, QB), I32))
        return jnp.sum(acc, axis=0, keepdims=True)

    c0 = count(lambda kk, j: kk >= 0)
    v0 = jnp.where(c0 >= topk, 0, INT_MIN).astype(I32)
    n0 = jnp.where(c0 >= topk, c0, nj * KB).astype(I32)

    def vbit(t, carry):
        v, n_ge = carry
        cand = v + lax.shift_left(jnp.int32(1), 30 - t)
        c = count(lambda kk, j: kk >= cand)
        return jnp.where(c >= topk, cand, v), jnp.where(c >= topk, c, n_ge)

    vth, n_ge = lax.fori_loop(0, 31, vbit, (v0, n0))
    excess_ties = jnp.max(((n_ge > topk) & (vth > INT_MIN)).astype(I32)) > 0

    def tie_break():
        need = topk - count(lambda kk, j: kk > vth)

        def jbit(t, jv):
            cand = jv + lax.shift_left(jnp.int32(1), pos_bits - 1 - t)
            c = count(lambda kk, j: (kk == vth) & (j * KB + krow < cand))
            return jnp.where(c < need, cand, jv)
        return lax.fori_loop(0, pos_bits, jbit, jnp.zeros((1, QB), I32))

    jth = lax.cond(excess_ties, tie_break, lambda: jnp.full((1, QB), (1 << pos_bits) - 1, I32))

    def mask_chunk(j, carry):
        key = keys_ref[j]
        kpos = j * KB + krow
        sel = (key >= vth) & ((key > vth) | (kpos <= jth)) & (kpos <= qpos)
        mask_ref[j] = jnp.transpose(sel.astype(F32))
        return carry

    lax.fori_loop(0, nj, mask_chunk, 0)

    q_all = qcat_ref[0].reshape(DSA_HEADS * QB, QCAT)
    q_all = jnp.concatenate([q_all, jnp.zeros((DSA_HEADS * QB, KCAT - QCAT), BF16)], axis=1)
    hpg = DSA_HEADS // ATT_GROUPS
    rows = hpg * QB

    def masked_scores(g, kc, msk):
        s = _dg(q_all[g * rows:(g + 1) * rows], kc, NT).reshape(hpg, QB, KB)
        return jnp.where(msk[None], s, -jnp.inf)

    def lane_max(ref, g, s):
        m = ref[g * rows:(g + 1) * rows, :].reshape(hpg, QB, LANES)
        for c in range(nsub):
            m = jnp.maximum(m, s[:, :, c * LANES:(c + 1) * LANES])
        ref[g * rows:(g + 1) * rows, :] = m.reshape(rows, LANES)

    def row_max_pass(n_chunks):
        mx_ref[...] = jnp.full_like(mx_ref, -jnp.inf)

        def row_max(j, carry):
            kc = kcat_ref[0, pl.ds(pl.multiple_of(j * KB, KB), KB), :]
            msk = mask_ref[j] > 0.5
            for g in range(ATT_GROUPS):
                lane_max(mx_ref, g, masked_scores(g, kc, msk))
            return carry

        lax.fori_loop(0, n_chunks, row_max, 0)
        mx_ref[...] = jnp.broadcast_to(jnp.max(mx_ref[...], axis=-1, keepdims=True), mx_ref.shape)

    def attend_pass(track):
        acc_ref[...] = jnp.zeros_like(acc_ref)
        if track:
            top_ref[...] = jnp.full_like(top_ref, -jnp.inf)

        def attend(j, carry):
            kc = kcat_ref[0, pl.ds(pl.multiple_of(j * KB, KB), KB), :]
            val = jnp.concatenate([kc[:, :KV_RANK], jnp.ones((KB, KCAT - KV_RANK), BF16)], axis=1)
            msk = mask_ref[j] > 0.5
            for g in range(ATT_GROUPS):
                s = masked_scores(g, kc, msk)
                if track:
                    lane_max(top_ref, g, s)
                m = mx_ref[g * rows:(g + 1) * rows, :].reshape(hpg, QB, LANES)
                p = jnp.concatenate([jnp.exp2(s[:, :, c * LANES:(c + 1) * LANES] - m) for c in range(nsub)], axis=-1)
                acc_ref[g * rows:(g + 1) * rows, :] += _dg(p.astype(BF16).reshape(rows, KB), val)
            return carry

        lax.fori_loop(0, nj, attend, 0)

    row_max_pass(1)
    attend_pass(track=True)
    covered = top_ref[...] - mx_ref[...] <= SHIFT_SLACK
    uncovered = jnp.max(jnp.where(covered, 0, 1)) > 0

    @pl.when(uncovered)
    def _():
        row_max_pass(nj)
        attend_pass(track=False)

    o_lat = acc_ref[:, :KV_RANK] / acc_ref[:, KV_RANK:]
    y_ref[0] = jnp.concatenate([_mm(o_lat[hd * QB:(hd + 1) * QB], wuv_ref[hd]) for hd in range(DSA_HEADS)],
                               axis=-1).astype(BF16)


def _dsa(qcat, kcat, qidx, kidx, wh, wuv, topk):
    B, _, Tp, _ = qcat.shape
    Tk = kcat.shape[1]
    pos_bits = max(1, (Tk - 1).bit_length())
    qh = lambda n, c: pl.BlockSpec((1, n, QB, c), lambda b, i: (b, 0, i, 0))
    whole = lambda c: pl.BlockSpec((1, Tk, c), lambda b, i: (b, 0, 0))
    return pl.pallas_call(
        functools.partial(_dsa_kernel, topk=topk, pos_bits=pos_bits),
        grid=(B, Tp // QB),
        in_specs=[qh(DSA_HEADS, QCAT), qh(IDX_HEADS, IDX3), pl.BlockSpec((1, IDX_HEADS, QB), lambda b, i: (b, 0, i)),
                  whole(KCAT), whole(IDX3), _const(wuv.shape)],
        out_specs=pl.BlockSpec((1, QB, DSA_HEADS * V_DIM), lambda b, i: (b, i, 0)),
        out_shape=jax.ShapeDtypeStruct((B, Tp, DSA_HEADS * V_DIM), BF16),
        scratch_shapes=[pltpu.VMEM((Tk // KB, KB, QB), I32),
                        pltpu.VMEM((Tk // KB, QB, KB), F32),
                        pltpu.VMEM((DSA_HEADS * QB, LANES), F32),
                        pltpu.VMEM((DSA_HEADS * QB, LANES), F32),
                        pltpu.VMEM((DSA_HEADS * QB, KCAT), F32)],
        compiler_params=_cparams(("arbitrary", "arbitrary")),
        name="dsa_attn",
    )(qcat, qidx, wh, kcat, kidx, wuv)


def _merge_kernel(h_ref, gates_ref, ya_ref, yb_ref, yc_ref, pa_ref, pb_ref, pc_ref, wo_ref, o_ref, *, d):
    za = _dg(ya_ref[0], pa_ref[...])
    zb = _dg(yb_ref[0, 0], pb_ref[0:GRP, :])
    for j in range(1, yb_ref.shape[1]):
        zb = zb + _dg(yb_ref[0, j], pb_ref[j * GRP:(j + 1) * GRP, :])
    zc = _dg(yc_ref[0], pc_ref[...])
    merged = gates_ref[0, :, :d] * za + gates_ref[0, :, d:2 * d] * zb + gates_ref[0, :, 2 * d:] * zc
    o_ref[0] = h_ref[0] + _mm(merged, wo_ref[...])


def _merge(h, gates, ya, yb, yc, pa, pb, pc, wo, tb):
    B, Tp, D = h.shape
    row = lambda c: pl.BlockSpec((1, tb, c), lambda b, t: (b, t, 0))
    return pl.pallas_call(
        functools.partial(_merge_kernel, d=D),
        grid=(B, Tp // tb),
        in_specs=[row(D), row(gates.shape[2]), row(ya.shape[2]),
                  pl.BlockSpec((1, yb.shape[1], tb, GRP), lambda b, t: (b, 0, t, 0)), row(yc.shape[2])]
                 + [_const(w.shape) for w in (pa, pb, pc, wo)],
        out_specs=row(D),
        out_shape=jax.ShapeDtypeStruct((B, Tp, D), F32),
        compiler_params=_cparams(("arbitrary", "arbitrary")),
        name="merge",
    )(h, gates, ya, yb, yc, pa, pb, pc, wo)


def _first_index_of_max(x, valid, idx):
    big = jnp.int32(1 << 20)
    mx = jnp.max(jnp.where(valid, x, -jnp.inf), axis=0, keepdims=True)
    first = jnp.min(jnp.where(valid & (x == mx), idx, big), axis=0, keepdims=True)
    return mx, first


def _moe_kernel(h_ref, gn_ref, wr_ref, br_ref, w13_ref, w2_ref, gf_ref, o_ref,
                hn_ref, comb_ref, *, hidden, final):
    e = pl.program_id(1)

    @pl.when(e == 0)
    def _():
        h = h_ref[...]
        hn = _rms(h, gn_ref[...])
        hn_ref[...] = hn.astype(BF16)
        lt = jnp.transpose(_mm3(hn, wr_ref[...]) + br_ref[...])
        gidx = lax.broadcasted_iota(I32, (ROUTER_EXP0, lt.shape[1]), 0)
        is_grp = gidx < N_GROUPS
        gl = lt[:ROUTER_EXP0]
        gmax, gsel = _first_index_of_max(gl, is_grp, gidx)
        p_grp = 1.0 / jnp.sum(jnp.where(is_grp, jnp.exp(gl - gmax), 0.0), axis=0, keepdims=True)
        el = lt[ROUTER_EXP0:ROUTER_EXP0 + N_EXPERTS]
        eidx = lax.broadcasted_iota(I32, el.shape, 0)
        in_grp = (eidx // EPG) == gsel
        t1, i1 = _first_index_of_max(el, in_grp, eidx)
        t2, i2 = _first_index_of_max(el, in_grp & (eidx != i1), eidx)
        e2 = jnp.exp(t2 - t1)
        w1 = p_grp / (1.0 + e2)
        w2 = p_grp * e2 / (1.0 + e2)
        comb = jnp.where(eidx == i1, w1, 0.0) + jnp.where(eidx == i2, w2, 0.0)
        comb_ref[...] = jnp.transpose(
            jnp.concatenate([comb, jnp.zeros((LANES - N_EXPERTS, comb.shape[1]), F32)], axis=0))
        o_ref[...] = h

    hn = hn_ref[...]
    lane = lax.broadcasted_iota(I32, comb_ref.shape, 1)
    ce = jnp.sum(jnp.where(lane == e, comb_ref[...], 0.0), axis=-1, keepdims=True)
    x13 = _dg(hn, w13_ref[0])
    x1, x3 = x13[:, :hidden], x13[:, hidden:]
    hid = x1 * _sigmoid(x1) * x3 * ce
    o_ref[...] += _mm(hid, w2_ref[0])

    if final:
        @pl.when(e == pl.num_programs(1) - 1)
        def _():
            o_ref[...] = _rms(o_ref[...], gf_ref[...])


def _moe(h2, gn, wr, br, w13, w2, gf, final, tb):
    n, D = h2.shape
    E, _, H2 = w13.shape
    row = pl.BlockSpec((tb, D), lambda r, e: (r, 0))
    return pl.pallas_call(
        functools.partial(_moe_kernel, hidden=H2 // 2, final=final),
        grid=(n // tb, E),
        in_specs=[row] + [_const(a.shape) for a in (gn, wr, br)]
                 + [pl.BlockSpec((1, D, H2), lambda r, e: (e, 0, 0)),
                    pl.BlockSpec((1, H2 // 2, D), lambda r, e: (e, 0, 0)), _const(gf.shape)],
        out_specs=row,
        out_shape=jax.ShapeDtypeStruct((n, D), F32),
        scratch_shapes=[pltpu.VMEM((tb, D), BF16), pltpu.VMEM((tb, LANES), F32)],
        compiler_params=_cparams(("arbitrary", "arbitrary")),
        name="moe",
    )(h2, gn, wr, br, w13, w2, gf)


def _rope_tables(tp):
    inv = ROPE_THETA ** (-jnp.arange(0, ROPE, 2, dtype=F32) / ROPE)
    ang = jnp.arange(tp, dtype=F32)[:, None] * inv[None, :]
    cos, sin = jnp.cos(ang), jnp.sin(ang)
    c32 = jnp.concatenate([cos, cos], axis=-1)
    s32 = jnp.concatenate([-sin, sin], axis=-1)
    one32, zero32 = jnp.ones_like(c32), jnp.zeros_like(c32)
    cq, sq = jnp.tile(c32, (1, DSA_HEADS)), jnp.tile(s32, (1, DSA_HEADS))
    ci = jnp.tile(jnp.concatenate([c32, one32], axis=-1), (1, IDX_HEADS))
    si = jnp.tile(jnp.concatenate([s32, zero32], axis=-1), (1, IDX_HEADS))
    ck = jnp.concatenate([c32, c32, one32, one32], axis=-1)
    sk = jnp.concatenate([s32, s32, zero32, zero32], axis=-1)
    return cq, sq, ci, si, ck, sk


def _pad_cols(w, n):
    return jnp.pad(w, ((0, 0), (0, n - w.shape[1])))


def kernel(x, meta_tokens, norm_mix, w_in, mu_shift, pool_w, pool_scale, rwkv_w0, rwkv_w2, rwkv_a0, rwkv_a2, rwkv_g2, rwkv_k_k, rwkv_k_a, rwkv_r_k, rwkv_ln_w, rwkv_ln_b, vres_w_down, vres_mu, vres_v0, vres_v2, mla_q_norm, mla_kv_norm, mla_w_uq, mla_w_uk, mla_w_uv, idx_w_q, idx_k_norm_w, idx_k_norm_b, w_proj_a, w_proj_b, w_proj_c, w_out, norm_ffn, router_w_group, router_b_group, router_w_expert, router_b_expert, expert_w1, expert_w3, expert_w2, norm_final):
    B, S, D = x.shape
    depth = w_in.shape[0]
    T = S + N_META
    Tp = -(-T // LANES) * LANES
    topk = min(MAX_TOPK, S // 4)
    c_a = pool_scale.shape[1]
    c_b = rwkv_w0.shape[1]
    c_shift = mu_shift.shape[1]
    npair = c_b // GRP
    o0 = 3 * D
    o1 = o0 + c_a
    o2 = o1 + c_shift
    c_dsa = Q_RANK + KV_RANK + ROPE + IDX_DIM + IDX_HEADS
    o3 = o2 + c_dsa
    nr = -(-(c_shift + VRES_RANK) // LANES) * LANES
    nd = -(-c_dsa // LANES) * LANES
    tb_in = _row_block(Tp, 576)
    tb_row = _row_block(Tp, 576)
    tb_moe = _row_block(B * Tp, 1152, LANES)
    row1 = lambda v: v.reshape(1, -1)

    h = jnp.concatenate([jnp.broadcast_to(meta_tokens[None].astype(x.dtype), (B, N_META, D)), x,
                         jnp.zeros((B, Tp - T, D), x.dtype)], axis=1)
    tables = _rope_tables(Tp)
    v_first = None
    for l in range(depth):
        w_l = w_in[l]
        w_r = w_l[:, o1:o2]
        mu = mu_shift[l]
        if l > 0:
            w_r = jnp.concatenate([w_r, vres_w_down[l - 1]], axis=1)
            mu = jnp.concatenate([mu, vres_mu[l - 1]])
        gates, ya, pb, pc = _in_proj(
            h, row1(norm_mix[l]), w_l[:, :o0].astype(BF16), w_l[:, o0:o1].astype(BF16),
            _pad_cols(w_r, nr).astype(BF16), _pad_cols(w_l[:, o2:o3], nd).astype(BF16),
            _pad_cols(row1(mu), nr), pool_w[l].astype(BF16), row1(pool_scale[l]), tb_in)

        prep_params = [row1(rwkv_w0[l]), rwkv_w2[l], row1(rwkv_a0[l]), rwkv_a2[l], rwkv_g2[l],
                       row1(rwkv_k_k[l]), row1(rwkv_k_a[l])]
        if l > 0:
            prep_params += [row1(vres_v0[l - 1]), vres_v2[l - 1]]
        r_, lw_, k_, v_, a_, b_, g_ = _rwkv_prep(pb, v_first if l > 0 else None, prep_params, c_b, tb_in)
        if l == 0:
            v_first = v_
        yb = _rwkv_scan(r_, lw_, k_, v_, a_, b_, g_, rwkv_ln_w[l].reshape(npair, 1, GRP),
                        rwkv_ln_b[l].reshape(npair, 1, GRP), rwkv_r_k[l].reshape(npair, 1, GRP))

        w_uq = mla_w_uq[l].reshape(Q_RANK, DSA_HEADS, NOPE + ROPE)
        idx_slab = lambda v: jnp.pad(v, (ROPE, LANES - ROPE - IDX_DIM)).reshape(1, LANES)
        dsa_params = [row1(mla_q_norm[l]), row1(mla_kv_norm[l]),
                      w_uq[:, :, :NOPE].reshape(Q_RANK, DSA_HEADS * NOPE).astype(BF16),
                      w_uq[:, :, NOPE:].reshape(Q_RANK, DSA_HEADS * ROPE).astype(BF16),
                      mla_w_uk[l].astype(BF16), idx_w_q[l], idx_slab(idx_k_norm_w[l]), idx_slab(idx_k_norm_b[l])]
        qcat, kcat, qidx, kidx, wh = _dsa_prep(pc, tables, dsa_params, tb_in)
        key_pad = ((0, 0), (0, -(-Tp // KB) * KB - Tp), (0, 0))
        yc = _dsa(qcat, jnp.pad(kcat, key_pad), qidx, jnp.pad(kidx, key_pad), jnp.swapaxes(wh, 1, 2),
                  mla_w_uv[l].astype(BF16), topk)

        h = _merge(h, gates, ya, yb, yc, w_proj_a[l].astype(BF16), w_proj_b[l].astype(BF16),
                   w_proj_c[l].astype(BF16), w_out[l].astype(BF16), tb_row)

        w13 = jnp.concatenate([expert_w1[l], expert_w3[l]], axis=-1).astype(BF16)
        gpad = ROUTER_EXP0 - N_GROUPS
        w_router = _pad_cols(jnp.concatenate([jnp.pad(router_w_group[l], ((0, 0), (0, gpad))),
                                              router_w_expert[l]], axis=1), LANES)
        b_router = _pad_cols(row1(jnp.concatenate([jnp.pad(router_b_group[l], (0, gpad)), router_b_expert[l]])),
                             LANES)
        h = _moe(h.reshape(B * Tp, D), row1(norm_ffn[l]), w_router, b_router, w13, expert_w2[l].astype(BF16),
                 row1(norm_final), l == depth - 1, tb_moe).reshape(B, Tp, D)
    return h[:, N_META:T]
```

```python
import functools
import math

import jax
import jax.numpy as jnp
from jax import lax
from jax.experimental import pallas as pl
from jax.experimental.pallas import tpu as pltpu

F32 = jnp.float32
BF16 = jnp.bfloat16
I32 = jnp.int32

N_META = 16
EPS = 1e-6
ROPE_THETA = 10000.0
POOL_WINDOWS = (2, 4, 8, 16)
POOL_HALO = 16
HEAD_DIM = 64
GN_EPS = 64e-5
DECAY_RANK = 64
A_RANK = 64
GATE_RANK = 128
VRES_RANK = 32
DSA_HEADS = 16
NOPE = 64
ROPE = 32
V_DIM = 64
Q_RANK = 256
KV_RANK = 128
IDX_HEADS = 8
IDX_DIM = 64
MAX_TOPK = 256
N_GROUPS = 4
EPG = 4
N_EXPERTS = N_GROUPS * EPG
ROUTER_EXP0 = 8

LANES = 128
VMEM_LIMIT = 56 * 1024 * 1024

CHUNK = 64
GRP_HEADS = 4
GRP = GRP_HEADS * HEAD_DIM
SCAN_CHUNKS = 8
INV = 2 * HEAD_DIM
QB = 128
KB = 512
ATT_GROUPS = 4
QCAT = KV_RANK + ROPE
KCAT = 2 * LANES
IDX3 = 3 * IDX_DIM
INT_MIN = -2147483648
SHIFT_SLACK = 100.0

NN = (((1,), (0,)), ((), ()))
NT = (((1,), (1,)), ((), ()))
TN = (((0,), (0,)), ((), ()))


def _dg(a, b, dims=NN):
    return lax.dot_general(a, b, dims, preferred_element_type=F32)


def _mm(a, b, dims=NN):
    return _dg(a.astype(BF16), b.astype(BF16), dims)


def _split(x):
    hi = x.astype(BF16)
    lo = (x - hi.astype(F32)).astype(BF16)
    return hi, lo


def _split3(x):
    a1 = x.astype(BF16)
    r1 = x - a1.astype(F32)
    a2 = r1.astype(BF16)
    a3 = (r1 - a2.astype(F32)).astype(BF16)
    return a1, a2, a3


def _mm3(a, b, dims=NN):
    ah, al = _split(a)
    bh, bl = _split(b)
    return _dg(ah, bh, dims) + (_dg(al, bh, dims) + _dg(ah, bl, dims))


def _mm_rhs_exact(a, m, dims=NN):
    ah, al = _split(a)
    return _dg(ah, m, dims) + _dg(al, m, dims)


def _cparams(sem):
    return pltpu.CompilerParams(dimension_semantics=sem, vmem_limit_bytes=VMEM_LIMIT)


def _row_block(n, target, mult=16):
    best = None
    for d in range(mult, min(n, target) + 1, mult):
        if n % d == 0:
            best = d
    assert best is not None, n
    return best


def _const(shape):
    nd = len(shape)
    return pl.BlockSpec(shape, lambda *_: (0,) * nd, pipeline_mode=pl.Buffered(1))


def _sigmoid(x):
    return 1.0 / (1.0 + jnp.exp(-x))


def _softplus(x):
    return jnp.maximum(x, 0.0) + jnp.log(1.0 + jnp.exp(-jnp.abs(x)))


def _rms(x, g):
    return x * lax.rsqrt(jnp.mean(x * x, axis=-1, keepdims=True) + EPS) * g


def _in_kernel(h_ref, g_ref, wg_ref, wp_ref, wr_ref, wd_ref, mu_ref, poolw_ref, pscale_ref,
               gates_ref, ya_ref, pb_ref, pc_ref, carry_pool, carry_r, *, tb):
    t = pl.program_id(1)

    @pl.when(t == 0)
    def _():
        carry_pool[...] = jnp.zeros_like(carry_pool)
        carry_r[...] = jnp.zeros_like(carry_r)

    u = _rms(h_ref[0], g_ref[...]).astype(BF16)

    gates_ref[0] = _sigmoid(_dg(u, wg_ref[...]))
    pc_ref[0] = _dg(u, wd_ref[...])

    p_r = _dg(u, wr_ref[...])
    row = lax.broadcasted_iota(I32, (tb, 1), 0)
    prev = jnp.where(row == 0, carry_r[7:8, :], pltpu.roll(p_r, 1, axis=0))
    pb_ref[0] = p_r + (prev - p_r) * mu_ref[...]
    carry_r[...] = p_r[tb - 8:, :]

    p_pool = _dg(u, wp_ref[...])
    ext = jnp.concatenate([carry_pool[...], p_pool], axis=0)
    carry_pool[...] = p_pool[tb - POOL_HALO:, :]
    tpos = (t * tb + row).astype(F32)
    gdim = p_pool.shape[1] // len(POOL_WINDOWS)
    ys = []
    for g, w in enumerate(POOL_WINDOWS):
        s, step = ext[:, g * gdim:(g + 1) * gdim], 1
        while step < w:
            s = s + pltpu.roll(s, step, axis=0)
            step *= 2
        cnt = jnp.minimum(tpos + 1.0, float(w))
        pooled = s[POOL_HALO:, :] / cnt - p_pool[:, g * gdim:(g + 1) * gdim]
        ys.append(_mm(pooled, poolw_ref[g]))
    ya_ref[0] = (jnp.concatenate(ys, axis=-1) * pscale_ref[...]).astype(BF16)


def _in_proj(h, g, wg, wp, wr, wd, mu, poolw, pscale, tb):
    B, Tp, D = h.shape
    blk = lambda c: pl.BlockSpec((1, tb, c), lambda b, t: (b, t, 0))
    widths = (wg.shape[1], wp.shape[1], wr.shape[1], wd.shape[1])
    return pl.pallas_call(
        functools.partial(_in_kernel, tb=tb),
        grid=(B, Tp // tb),
        in_specs=[blk(D)] + [_const(a.shape) for a in (g, wg, wp, wr, wd, mu, poolw, pscale)],
        out_specs=[blk(c) for c in widths],
        out_shape=[jax.ShapeDtypeStruct((B, Tp, widths[0]), F32),
                   jax.ShapeDtypeStruct((B, Tp, widths[1]), BF16),
                   jax.ShapeDtypeStruct((B, Tp, widths[2]), F32),
                   jax.ShapeDtypeStruct((B, Tp, widths[3]), F32)],
        scratch_shapes=[pltpu.VMEM((POOL_HALO, widths[1]), F32), pltpu.VMEM((8, widths[2]), F32)],
        compiler_params=_cparams(("arbitrary", "arbitrary")),
        name="in_proj",
    )(h, g, wg, wp, wr, wd, mu, poolw, pscale)


def _rwkv_prep_kernel(*refs, c, use_vres):
    if use_vres:
        (pb_ref, vf_ref, w0_ref, w2_ref, a0_ref, a2_ref, g2_ref, kk_ref, ka_ref, v0_ref, v2_ref,
         r_o, lw_o, k_o, v_o, a_o, b_o, g_o) = refs
    else:
        (pb_ref, w0_ref, w2_ref, a0_ref, a2_ref, g2_ref, kk_ref, ka_ref,
         r_o, lw_o, k_o, v_o, a_o, b_o, g_o) = refs
    o = 3 * c
    wd = pb_ref[0, :, o:o + DECAY_RANK]
    ad = pb_ref[0, :, o + DECAY_RANK:o + DECAY_RANK + A_RANK]
    gd = pb_ref[0, :, o + DECAY_RANK + A_RANK:o + DECAY_RANK + A_RANK + GATE_RANK]
    w_log = -_softplus(-(w0_ref[...] + _mm3(jnp.tanh(wd), w2_ref[...]))) - 0.5
    lw = -jnp.exp(w_log)
    a = _sigmoid(a0_ref[...] + _mm3(ad, a2_ref[...]))
    g = _mm3(_sigmoid(gd), g2_ref[...])
    if use_vres:
        vd = pb_ref[0, :, o + DECAY_RANK + A_RANK + GATE_RANK:o + DECAY_RANK + A_RANK + GATE_RANK + VRES_RANK]
        vmix = _sigmoid(v0_ref[...] + _mm3(vd, v2_ref[...]))
    ri = lax.broadcasted_iota(I32, (GRP, GRP), 0) // HEAD_DIM
    ci = lax.broadcasted_iota(I32, (GRP, GRP), 1) // HEAD_DIM
    ones_blk = (ri == ci).astype(BF16)
    for j in range(c // GRP):
        sl = slice(j * GRP, (j + 1) * GRP)
        r = pb_ref[0, :, j * GRP:(j + 1) * GRP]
        k = pb_ref[0, :, c + j * GRP:c + (j + 1) * GRP]
        v = pb_ref[0, :, 2 * c + j * GRP:2 * c + (j + 1) * GRP]
        if use_vres:
            v = v + (vf_ref[0, j] - v) * vmix[:, sl]
        aj = a[:, sl]
        kk = k * kk_ref[:, sl]
        nrm = jnp.sqrt(_mm_rhs_exact(kk * kk, ones_blk))
        kk = kk / jnp.maximum(nrm, 1e-12)
        r_o[0, j] = r
        lw_o[0, j] = lw[:, sl]
        k_o[0, j] = k * (1.0 + (aj - 1.0) * ka_ref[:, sl])
        v_o[0, j] = v
        a_o[0, j] = -kk
        b_o[0, j] = kk * aj
        g_o[0, j] = g[:, sl]


def _rwkv_prep(pb, v_first, params, c, tb):
    B, Tp, NR = pb.shape
    npair = c // GRP
    use_vres = v_first is not None
    pair_blk = pl.BlockSpec((1, npair, tb, GRP), lambda b, t: (b, 0, t, 0))
    ins = [pb] + ([v_first] if use_vres else []) + list(params)
    in_specs = ([pl.BlockSpec((1, tb, NR), lambda b, t: (b, t, 0))] + ([pair_blk] if use_vres else [])
                + [_const(p.shape) for p in params])
    return pl.pallas_call(
        functools.partial(_rwkv_prep_kernel, c=c, use_vres=use_vres),
        grid=(B, Tp // tb),
        in_specs=in_specs,
        out_specs=[pair_blk] * 7,
        out_shape=[jax.ShapeDtypeStruct((B, npair, Tp, GRP), F32)] * 7,
        compiler_params=_cparams(("arbitrary", "arbitrary")),
        name="rwkv_prep",
    )(*ins)


def _scan_kernel(r_ref, lw_ref, k_ref, v_ref, a_ref, b_ref, g_ref, lnw_ref, lnb_ref, rk_ref, y_ref,
                 *, n_chunks, unroll):
    C = CHUNK

    def rows(c_):
        return pl.ds(c_ * C, C) if isinstance(c_, int) else pl.ds(pl.multiple_of(c_ * C, C), C)

    ri = lax.broadcasted_iota(I32, (GRP, GRP), 0)
    ci = lax.broadcasted_iota(I32, (GRP, GRP), 1)
    same = (ri // C) == (ci // C)
    strict = same & ((ri % C) > (ci % C))
    incl = same & ((ri % C) >= (ci % C))
    eye = (ri == ci).astype(F32)
    ones2 = jnp.concatenate([same, same], axis=0).astype(BF16)
    tri = (lax.broadcasted_iota(I32, (C, C), 0) >= lax.broadcasted_iota(I32, (C, C), 1)).astype(BF16)
    tri3 = jnp.concatenate([tri, tri, tri], axis=1)
    lane_head = lax.broadcasted_iota(I32, (C, GRP), 1) // HEAD_DIM
    lnw, lnb, rk = lnw_ref[0], lnb_ref[0], rk_ref[0]

    def blockdiag(x):
        return jnp.concatenate([jnp.where(lane_head == hd, x, 0.0) for hd in range(GRP_HEADS)], axis=0)

    def each(f, *lists):
        return [f(*xs) for xs in zip(*lists)]

    def chunk_terms(chunk_ids):
        sls = [rows(c_) for c_ in chunk_ids]
        lw = [lw_ref[0, 0, sl, :] for sl in sls]
        cs = each(lambda x: _dg(tri3, jnp.concatenate(_split3(x), axis=0)), lw)
        p_in = each(jnp.exp, cs)
        p_inv = each(lambda x: jnp.exp(-x), cs)
        a_d = each(lambda sl, c, l: blockdiag(a_ref[0, 0, sl, :] * jnp.exp(c - l)), sls, cs, lw)
        r_d = each(lambda sl, p: blockdiag(r_ref[0, 0, sl, :] * p), sls, p_in)
        b_d = each(lambda sl, p: blockdiag(b_ref[0, 0, sl, :] * p).astype(BF16), sls, p_inv)
        k_d = each(lambda sl, p: blockdiag(k_ref[0, 0, sl, :] * p).astype(BF16), sls, p_inv)
        v_d = each(lambda sl: blockdiag(v_ref[0, 0, sl, :]).astype(BF16), sls)
        m = each(lambda a, r, b, k: _mm(jnp.concatenate([a, r], axis=0), jnp.concatenate([b, k], axis=0), NT),
                 a_d, r_d, b_d, k_d)
        l_ab = each(lambda x: jnp.where(strict, x[:GRP, :GRP], 0.0), m)
        lk = each(lambda x: jnp.concatenate([jnp.where(strict, x[:GRP, GRP:], 0.0),
                                             jnp.where(incl, x[GRP:, GRP:], 0.0)], axis=0), m)
        m_rb = each(lambda x: jnp.where(incl, x[GRP:, :GRP], 0.0), m)
        lv = each(_mm, lk, v_d)
        parts = [slice(q * INV, (q + 1) * INV) for q in range(GRP // INV)]
        eye_p = eye[:INV, :INV]
        lp = [x[q, q] for x in l_ab for q in parts]
        t = each(lambda x: eye_p + x, lp)
        lp = each(lambda x: _mm(x, x), lp)
        step = 2
        while step < C:
            if 2 * step < C:
                x = each(lambda p, q: _mm(p, jnp.concatenate([q, p], axis=1)), lp, t)
                t = each(lambda q, y: q + y[:, :INV], t, x)
                lp = each(lambda y: y[:, INV:], x)
            else:
                t = each(lambda q, p: q + _mm(p, q), t, lp)
            step *= 2
        rhs = each(lambda a, y: jnp.concatenate([a, y[:GRP]], axis=1), a_d, lv)
        npart = len(parts)
        w = [jnp.concatenate([_mm(t[i * npart + qi], rhs[i][q]) for qi, q in enumerate(parts)], axis=0)
             for i in range(len(rhs))]
        rbw = each(_mm, m_rb, w)
        r2 = each(lambda r, y: r + y[:, :GRP], r_d, rbw)
        o0 = each(lambda y, z: y[:, GRP:] + z[GRP:], rbw, lv)
        tw = each(lambda x, b: _mm(x, b, TN), w, b_d)
        vk = each(lambda v, k: _mm(v, k, TN), v_d, k_d)
        gmat = each(lambda x, p: (eye + x[:GRP]) * p[C - 1:C, :], tw, p_in)
        hmat = each(lambda x, y, p: (x[GRP:] + y) * p[C - 1:C, :], tw, vk, p_in)
        return list(zip(r2, o0, gmat, hmat))

    def cat_split(x):
        hi, lo = _split(x)
        return jnp.concatenate([hi, lo], axis=1)

    def advance(chunk_ids, s):
        sls = [rows(c_) for c_ in chunk_ids]
        ods = []
        for r2, o0, gmat, hmat in chunk_terms(chunk_ids):
            ods.append(_mm(r2, s, NT) + o0)
            sh, slo = _split(s)
            gh, glo = _split(gmat)
            s = _dg(jnp.concatenate([sh, slo, sh], axis=1), jnp.concatenate([gh, gh, glo], axis=0)) + hmat
        o = each(lambda od: sum(od[hd * C:(hd + 1) * C] for hd in range(1, GRP_HEADS)) + od[:C], ods)
        sums = each(lambda sl, o_: _dg(cat_split(jnp.concatenate(
            [o_, r_ref[0, 0, sl, :] * k_ref[0, 0, sl, :] * rk], axis=0)), ones2), sls, o)
        d = each(lambda o_, z: o_ - z[:C] * (1.0 / HEAD_DIM), o, sums)
        var = each(lambda x: _dg(cat_split(x * x), ones2) * (1.0 / HEAD_DIM), d)
        for sl, d_, var_, z in zip(sls, d, var, sums):
            on = d_ * lax.rsqrt(var_ + GN_EPS) * lnw + lnb
            y_ref[0, 0, sl, :] = ((on + z[C:] * v_ref[0, 0, sl, :]) * g_ref[0, 0, sl, :]).astype(BF16)
        return s

    n_main = n_chunks // unroll
    s = lax.fori_loop(0, n_main, lambda it, s_: advance([it * unroll + u for u in range(unroll)], s_),
                      jnp.zeros((GRP, GRP), F32))
    if n_chunks % unroll:
        advance(list(range(n_main * unroll, n_chunks)), s)


def _rwkv_scan(r, lw, k, v, a, b, g, lnw, lnb, rk):
    B, ngrp, Tp, _ = r.shape
    n_chunks = Tp // CHUNK
    unroll = min(SCAN_CHUNKS, n_chunks)
    seq = pl.BlockSpec((1, 1, Tp, GRP), lambda bi, j: (bi, j, 0, 0))
    par = pl.BlockSpec((1, 1, GRP), lambda bi, j: (j, 0, 0))
    return pl.pallas_call(
        functools.partial(_scan_kernel, n_chunks=n_chunks, unroll=unroll),
        grid=(B, ngrp),
        in_specs=[seq] * 7 + [par] * 3,
        out_specs=seq,
        out_shape=jax.ShapeDtypeStruct((B, ngrp, Tp, GRP), BF16),
        compiler_params=_cparams(("arbitrary", "arbitrary")),
        name="rwkv_scan",
    )(r, lw, k, v, a, b, g, lnw, lnb, rk)


def _swap16(x):
    n = x.shape[-1]
    lane = lax.broadcasted_iota(I32, x.shape, x.ndim - 1)
    return jnp.where((lane % 32) < 16, pltpu.roll(x, n - 16, axis=x.ndim - 1), pltpu.roll(x, 16, axis=x.ndim - 1))


def _dsa_prep_kernel(pc_ref, cq_t, sq_t, ci_t, si_t, ck_t, sk_t, gq_ref, gkv_ref, wn_ref, wr_ref, wuk_ref,
                     wqi_ref, lnw_ref, lnb_ref, qcat_o, kcat_o, qidx_o, kidx_o, wh_o, *, scale):
    cq = _rms(pc_ref[0, :, :Q_RANK], gq_ref[...])
    ckv = _rms(pc_ref[0, :, Q_RANK:Q_RANK + KV_RANK], gkv_ref[...])
    slab = pc_ref[0, :, Q_RANK + KV_RANK:Q_RANK + KV_RANK + LANES]
    lane = lax.broadcasted_iota(I32, slab.shape, 1)
    in_idx = (lane >= ROPE) & (lane < ROPE + IDX_DIM)
    mean = jnp.sum(jnp.where(in_idx, slab, 0.0), axis=-1, keepdims=True) * (1.0 / IDX_DIM)
    dev = jnp.where(in_idx, slab - mean, 0.0)
    var = jnp.sum(dev * dev, axis=-1, keepdims=True) * (1.0 / IDX_DIM)
    kn = dev * lax.rsqrt(var + EPS) * lnw_ref[...] + lnb_ref[...]
    kslab = jnp.where(lane < ROPE, slab, kn)
    kslab = kslab * ck_t[...] + _swap16(kslab) * sk_t[...]
    tail = jnp.where(lane < ROPE, kslab, 0.0)
    kcat_o[0] = jnp.concatenate([ckv, tail], axis=-1).astype(BF16)
    khi, klo = _split(kslab[:, ROPE:ROPE + IDX_DIM])
    kidx_o[0] = jnp.concatenate([khi, khi, klo], axis=-1)
    wh_o[0] = slab[:, ROPE + IDX_DIM:ROPE + IDX_DIM + IDX_HEADS] * (IDX_HEADS ** -0.5 * IDX_DIM ** -0.5)

    q_nope = _mm(cq, wn_ref[...])
    q_rope = _mm(cq, wr_ref[...])
    q_rope = q_rope * cq_t[...] + _swap16(q_rope) * sq_t[...]
    q_idx = _mm3(cq, wqi_ref[...])
    q_idx = q_idx * ci_t[...] + _swap16(q_idx) * si_t[...]
    for hd in range(DSA_HEADS):
        q_lat = _mm(q_nope[:, hd * NOPE:(hd + 1) * NOPE], wuk_ref[hd], NT)
        qc = jnp.concatenate([q_lat, q_rope[:, hd * ROPE:(hd + 1) * ROPE]], axis=-1) * scale
        qcat_o[0, hd] = qc.astype(BF16)
    for hd in range(IDX_HEADS):
        qhi, qlo = _split(q_idx[:, hd * IDX_DIM:(hd + 1) * IDX_DIM])
        qidx_o[0, hd] = jnp.concatenate([qhi, qlo, qhi], axis=-1)


def _dsa_prep(pc, tables, params, tb):
    B, Tp, ND = pc.shape
    row = lambda c: pl.BlockSpec((1, tb, c), lambda b, t: (b, t, 0))
    tab = lambda a: pl.BlockSpec((tb, a.shape[1]), lambda b, t: (t, 0))
    head = lambda n, c: pl.BlockSpec((1, n, tb, c), lambda b, t: (b, 0, t, 0))
    scale = float((NOPE + ROPE) ** -0.5 * math.log2(math.e))
    return pl.pallas_call(
        functools.partial(_dsa_prep_kernel, scale=scale),
        grid=(B, Tp // tb),
        in_specs=[row(ND)] + [tab(a) for a in tables] + [_const(p.shape) for p in params],
        out_specs=[head(DSA_HEADS, QCAT), row(KCAT), head(IDX_HEADS, IDX3), row(IDX3), row(IDX_HEADS)],
        out_shape=[jax.ShapeDtypeStruct((B, DSA_HEADS, Tp, QCAT), BF16),
                   jax.ShapeDtypeStruct((B, Tp, KCAT), BF16),
                   jax.ShapeDtypeStruct((B, IDX_HEADS, Tp, IDX3), BF16),
                   jax.ShapeDtypeStruct((B, Tp, IDX3), BF16),
                   jax.ShapeDtypeStruct((B, Tp, IDX_HEADS), F32)],
        compiler_params=_cparams(("arbitrary", "arbitrary")),
        name="dsa_prep",
    )(pc, *tables, *params)


def _dsa_kernel(qcat_ref, qidx_ref, wh_ref, kcat_ref, kidx_ref, wuv_ref, y_ref,
                keys_ref, mask_ref, mx_ref, top_ref, acc_ref, *, topk, pos_bits):
    i = pl.program_id(1)
    nj = (i * QB + QB + KB - 1) // KB
    qb_per_chunk = KB // QB
    short_tail = (i % qb_per_chunk) < qb_per_chunk // 2
    def qpos(w):
        return i * QB + lax.broadcasted_iota(I32, (w, QB), 1)

    def kpos(j, w):
        return j * KB + lax.broadcasted_iota(I32, (w, QB), 0)

    q_idx = qidx_ref[0].reshape(IDX_HEADS * QB, IDX3)
    w_head = wh_ref[0]

    def tail_chunk(body, *args):
        return lax.cond(short_tail, lambda *a: body(nj - 1, KB // 2, *a), lambda *a: body(nj - 1, KB, *a), *args)

    def for_chunks(body, n_full=None, tail=True):
        def full(j, carry):
            body(j, KB)
            return carry
        lax.fori_loop(0, nj - 1 if n_full is None else n_full, full, 0)
        if tail is True:
            tail_chunk(body)
        else:
            pl.when(tail)(lambda: tail_chunk(body))

    def score_chunk(j, w):
        kc = kidx_ref[0, pl.ds(pl.multiple_of(j * KB, KB), w), :]
        s = jnp.maximum(_dg(kc, q_idx, NT), 0.0)
        tot = s[:, :QB] * w_head[0:1, :]
        for hd in range(1, IDX_HEADS):
            tot = tot + s[:, hd * QB:(hd + 1) * QB] * w_head[hd:hd + 1, :]
        tot = jnp.where(tot == 0.0, 0.0, tot)
        bits = pltpu.bitcast(tot, I32)
        key = jnp.where(bits < 0, bits ^ 0x7FFFFFFF, bits)
        keys_ref.at[j][:w, :] = jnp.where(kpos(j, w) <= qpos(w), key, INT_MIN)

    for_chunks(score_chunk)

    def count(pred):
        def step(j, w, acc):
            hit = pred(keys_ref.at[j][:w, :], j, w).astype(I32)
            return acc + jnp.sum(hit.reshape(w // 8, 8, QB), axis=0)
        acc = lax.fori_loop(0, nj - 1, lambda j, a: step(j, KB, a), jnp.zeros((8, QB), I32))
        return jnp.sum(tail_chunk(step, acc), axis=0, keepdims=True)

    c0 = count(lambda kk, j, w: kk >= 0)
    v0 = jnp.where(c0 >= topk, 0, INT_MIN).astype(I32)

    def vbit(t, carry):
        v, n_ge = carry
        cand = v + lax.shift_left(jnp.int32(1), 30 - t)
        c = count(lambda kk, j, w: kk >= cand)
        return jnp.where(c >= topk, cand, v), jnp.where(c >= topk, c, n_ge)

    vth, n_ge = lax.fori_loop(0, 31, vbit, (v0, c0))
    excess_ties = jnp.max(((n_ge > topk) & (vth > INT_MIN)).astype(I32)) > 0

    def tie_break():
        need = topk - count(lambda kk, j, w: kk > vth)

        def jbit(t, jv):
            cand = jv + lax.shift_left(jnp.int32(1), pos_bits - 1 - t)
            c = count(lambda kk, j, w: (kk == vth) & (kpos(j, w) < cand))
            return jnp.where(c < need, cand, jv)
        return lax.fori_loop(0, pos_bits, jbit, jnp.zeros((1, QB), I32))

    jth = lax.cond(excess_ties, tie_break, lambda: jnp.full((1, QB), (1 << pos_bits) - 1, I32))

    def mask_chunk(j, w):
        key = keys_ref.at[j][:w, :]
        kp = kpos(j, w)
        sel = (key >= vth) & ((key > vth) | (kp <= jth)) & (kp <= qpos(w))
        sel = sel.astype(F32)
        for c in range(w // LANES):
            mask_ref.at[j][:, c * LANES:(c + 1) * LANES] = jnp.transpose(sel[c * LANES:(c + 1) * LANES, :])

    for_chunks(mask_chunk)

    q_all = qcat_ref[0].reshape(DSA_HEADS * QB, QCAT)
    q_all = jnp.concatenate([q_all, jnp.zeros((DSA_HEADS * QB, KCAT - QCAT), BF16)], axis=1)
    hpg = DSA_HEADS // ATT_GROUPS
    rows = hpg * QB

    def masked_scores(g, kc, msk):
        s = _dg(q_all[g * rows:(g + 1) * rows], kc, NT).reshape(hpg, QB, kc.shape[0])
        return jnp.where(msk[None], s, -jnp.inf)

    def lane_max(ref, g, s):
        m = ref[g * rows:(g + 1) * rows, :].reshape(hpg, QB, LANES)
        for c in range(s.shape[-1] // LANES):
            m = jnp.maximum(m, s[:, :, c * LANES:(c + 1) * LANES])
        ref[g * rows:(g + 1) * rows, :] = m.reshape(rows, LANES)

    def row_max_pass(first_only):
        mx_ref[...] = jnp.full_like(mx_ref, -jnp.inf)

        def row_max(j, w):
            kc = kcat_ref[0, pl.ds(pl.multiple_of(j * KB, KB), w), :]
            msk = mask_ref.at[j][:, :w] > 0.5
            for g in range(ATT_GROUPS):
                lane_max(mx_ref, g, masked_scores(g, kc, msk))

        if first_only:
            for_chunks(row_max, n_full=jnp.minimum(1, nj - 1), tail=nj == 1)
        else:
            for_chunks(row_max)
        mx_ref[...] = jnp.broadcast_to(jnp.max(mx_ref[...], axis=-1, keepdims=True), mx_ref.shape)

    def attend_pass(track):
        acc_ref[...] = jnp.zeros_like(acc_ref)
        if track:
            top_ref[...] = jnp.full_like(top_ref, -jnp.inf)

        def attend(j, w):
            kc = kcat_ref[0, pl.ds(pl.multiple_of(j * KB, KB), w), :]
            val = jnp.concatenate([kc[:, :KV_RANK], jnp.ones((w, KCAT - KV_RANK), BF16)], axis=1)
            msk = mask_ref.at[j][:, :w] > 0.5
            for g in range(ATT_GROUPS):
                s = masked_scores(g, kc, msk)
                if track:
                    lane_max(top_ref, g, s)
                m = mx_ref[g * rows:(g + 1) * rows, :].reshape(hpg, QB, LANES)
                p = jnp.concatenate([jnp.exp2(s[:, :, c * LANES:(c + 1) * LANES] - m)
                                     for c in range(w // LANES)], axis=-1)
                acc_ref[g * rows:(g + 1) * rows, :] += _dg(p.astype(BF16).reshape(rows, w), val)

        for_chunks(attend)

    row_max_pass(first_only=True)
    attend_pass(track=True)
    covered = top_ref[...] - mx_ref[...] <= SHIFT_SLACK
    uncovered = jnp.max(jnp.where(covered, 0, 1)) > 0

    @pl.when(uncovered)
    def _():
        row_max_pass(first_only=False)
        attend_pass(track=False)

    o_lat = acc_ref[:, :KV_RANK] / acc_ref[:, KV_RANK:]
    y_ref[0] = jnp.concatenate([_mm(o_lat[hd * QB:(hd + 1) * QB], wuv_ref[hd]) for hd in range(DSA_HEADS)],
                               axis=-1).astype(BF16)


def _dsa(qcat, kcat, qidx, kidx, wh, wuv, topk):
    B, _, Tp, _ = qcat.shape
    Tk = kcat.shape[1]
    pos_bits = max(1, (Tk - 1).bit_length())
    qh = lambda n, c: pl.BlockSpec((1, n, QB, c), lambda b, i: (b, 0, i, 0))
    whole = lambda c: pl.BlockSpec((1, Tk, c), lambda b, i: (b, 0, 0))
    return pl.pallas_call(
        functools.partial(_dsa_kernel, topk=topk, pos_bits=pos_bits),
        grid=(B, Tp // QB),
        in_specs=[qh(DSA_HEADS, QCAT), qh(IDX_HEADS, IDX3), pl.BlockSpec((1, IDX_HEADS, QB), lambda b, i: (b, 0, i)),
                  whole(KCAT), whole(IDX3), _const(wuv.shape)],
        out_specs=pl.BlockSpec((1, QB, DSA_HEADS * V_DIM), lambda b, i: (b, i, 0)),
        out_shape=jax.ShapeDtypeStruct((B, Tp, DSA_HEADS * V_DIM), BF16),
        scratch_shapes=[pltpu.VMEM((Tk // KB, KB, QB), I32),
                        pltpu.VMEM((Tk // KB, QB, KB), F32),
                        pltpu.VMEM((DSA_HEADS * QB, LANES), F32),
                        pltpu.VMEM((DSA_HEADS * QB, LANES), F32),
                        pltpu.VMEM((DSA_HEADS * QB, KCAT), F32)],
        compiler_params=_cparams(("arbitrary", "arbitrary")),
        name="dsa_attn",
    )(qcat, qidx, wh, kcat, kidx, wuv)


def _merge_kernel(h_ref, gates_ref, ya_ref, yb_ref, yc_ref, pa_ref, pb_ref, pc_ref, wo_ref, o_ref, *, d):
    za = _dg(ya_ref[0], pa_ref[...])
    zb = _dg(yb_ref[0, 0], pb_ref[0:GRP, :])
    for j in range(1, yb_ref.shape[1]):
        zb = zb + _dg(yb_ref[0, j], pb_ref[j * GRP:(j + 1) * GRP, :])
    zc = _dg(yc_ref[0], pc_ref[...])
    merged = gates_ref[0, :, :d] * za + gates_ref[0, :, d:2 * d] * zb + gates_ref[0, :, 2 * d:] * zc
    o_ref[0] = h_ref[0] + _mm(merged, wo_ref[...])


def _merge(h, gates, ya, yb, yc, pa, pb, pc, wo, tb):
    B, Tp, D = h.shape
    row = lambda c: pl.BlockSpec((1, tb, c), lambda b, t: (b, t, 0))
    return pl.pallas_call(
        functools.partial(_merge_kernel, d=D),
        grid=(B, Tp // tb),
        in_specs=[row(D), row(gates.shape[2]), row(ya.shape[2]),
                  pl.BlockSpec((1, yb.shape[1], tb, GRP), lambda b, t: (b, 0, t, 0)), row(yc.shape[2])]
                 + [_const(w.shape) for w in (pa, pb, pc, wo)],
        out_specs=row(D),
        out_shape=jax.ShapeDtypeStruct((B, Tp, D), F32),
        compiler_params=_cparams(("arbitrary", "arbitrary")),
        name="merge",
    )(h, gates, ya, yb, yc, pa, pb, pc, wo)


def _first_index_of_max(x, valid, idx):
    big = jnp.int32(1 << 20)
    mx = jnp.max(jnp.where(valid, x, -jnp.inf), axis=0, keepdims=True)
    first = jnp.min(jnp.where(valid & (x == mx), idx, big), axis=0, keepdims=True)
    return mx, first


def _moe_kernel(h_ref, gn_ref, wr_ref, br_ref, w13_ref, w2_ref, gf_ref, o_ref,
                hn_ref, comb_ref, *, hidden, final):
    e = pl.program_id(1)

    @pl.when(e == 0)
    def _():
        h = h_ref[...]
        hn = _rms(h, gn_ref[...])
        hn_ref[...] = hn.astype(BF16)
        lt = jnp.transpose(_mm3(hn, wr_ref[...]) + br_ref[...])
        gidx = lax.broadcasted_iota(I32, (ROUTER_EXP0, lt.shape[1]), 0)
        is_grp = gidx < N_GROUPS
        gl = lt[:ROUTER_EXP0]
        gmax, gsel = _first_index_of_max(gl, is_grp, gidx)
        p_grp = 1.0 / jnp.sum(jnp.where(is_grp, jnp.exp(gl - gmax), 0.0), axis=0, keepdims=True)
        el = lt[ROUTER_EXP0:ROUTER_EXP0 + N_EXPERTS]
        eidx = lax.broadcasted_iota(I32, el.shape, 0)
        in_grp = (eidx // EPG) == gsel
        t1, i1 = _first_index_of_max(el, in_grp, eidx)
        t2, i2 = _first_index_of_max(el, in_grp & (eidx != i1), eidx)
        e2 = jnp.exp(t2 - t1)
        w1 = p_grp / (1.0 + e2)
        w2 = p_grp * e2 / (1.0 + e2)
        comb = jnp.where(eidx == i1, w1, 0.0) + jnp.where(eidx == i2, w2, 0.0)
        comb_ref[...] = jnp.transpose(
            jnp.concatenate([comb, jnp.zeros((LANES - N_EXPERTS, comb.shape[1]), F32)], axis=0))
        o_ref[...] = h

    hn = hn_ref[...]
    lane = lax.broadcasted_iota(I32, comb_ref.shape, 1)
    ce = jnp.sum(jnp.where(lane == e, comb_ref[...], 0.0), axis=-1, keepdims=True)
    x13 = _dg(hn, w13_ref[0])
    x1, x3 = x13[:, :hidden], x13[:, hidden:]
    hid = x1 * _sigmoid(x1) * x3 * ce
    o_ref[...] += _mm(hid, w2_ref[0])

    if final:
        @pl.when(e == pl.num_programs(1) - 1)
        def _():
            o_ref[...] = _rms(o_ref[...], gf_ref[...])


def _moe(h2, gn, wr, br, w13, w2, gf, final, tb):
    n, D = h2.shape
    E, _, H2 = w13.shape
    row = pl.BlockSpec((tb, D), lambda r, e: (r, 0))
    return pl.pallas_call(
        functools.partial(_moe_kernel, hidden=H2 // 2, final=final),
        grid=(n // tb, E),
        in_specs=[row] + [_const(a.shape) for a in (gn, wr, br)]
                 + [pl.BlockSpec((1, D, H2), lambda r, e: (e, 0, 0)),
                    pl.BlockSpec((1, H2 // 2, D), lambda r, e: (e, 0, 0)), _const(gf.shape)],
        out_specs=row,
        out_shape=jax.ShapeDtypeStruct((n, D), F32),
        scratch_shapes=[pltpu.VMEM((tb, D), BF16), pltpu.VMEM((tb, LANES), F32)],
        compiler_params=_cparams(("arbitrary", "arbitrary")),
        name="moe",
    )(h2, gn, wr, br, w13, w2, gf)


def _rope_tables(tp):
    inv = ROPE_THETA ** (-jnp.arange(0, ROPE, 2, dtype=F32) / ROPE)
    ang = jnp.arange(tp, dtype=F32)[:, None] * inv[None, :]
    cos, sin = jnp.cos(ang), jnp.sin(ang)
    c32 = jnp.concatenate([cos, cos], axis=-1)
    s32 = jnp.concatenate([-sin, sin], axis=-1)
    one32, zero32 = jnp.ones_like(c32), jnp.zeros_like(c32)
    cq, sq = jnp.tile(c32, (1, DSA_HEADS)), jnp.tile(s32, (1, DSA_HEADS))
    ci = jnp.tile(jnp.concatenate([c32, one32], axis=-1), (1, IDX_HEADS))
    si = jnp.tile(jnp.concatenate([s32, zero32], axis=-1), (1, IDX_HEADS))
    ck = jnp.concatenate([c32, c32, one32, one32], axis=-1)
    sk = jnp.concatenate([s32, s32, zero32, zero32], axis=-1)
    return cq, sq, ci, si, ck, sk


def _pad_cols(w, n):
    return jnp.pad(w, ((0, 0), (0, n - w.shape[1])))


def kernel(x, meta_tokens, norm_mix, w_in, mu_shift, pool_w, pool_scale, rwkv_w0, rwkv_w2, rwkv_a0, rwkv_a2, rwkv_g2, rwkv_k_k, rwkv_k_a, rwkv_r_k, rwkv_ln_w, rwkv_ln_b, vres_w_down, vres_mu, vres_v0, vres_v2, mla_q_norm, mla_kv_norm, mla_w_uq, mla_w_uk, mla_w_uv, idx_w_q, idx_k_norm_w, idx_k_norm_b, w_proj_a, w_proj_b, w_proj_c, w_out, norm_ffn, router_w_group, router_b_group, router_w_expert, router_b_expert, expert_w1, expert_w3, expert_w2, norm_final):
    B, S, D = x.shape
    depth = w_in.shape[0]
    T = S + N_META
    Tp = -(-T // LANES) * LANES
    topk = min(MAX_TOPK, S // 4)
    c_a = pool_scale.shape[1]
    c_b = rwkv_w0.shape[1]
    c_shift = mu_shift.shape[1]
    npair = c_b // GRP
    o0 = 3 * D
    o1 = o0 + c_a
    o2 = o1 + c_shift
    c_dsa = Q_RANK + KV_RANK + ROPE + IDX_DIM + IDX_HEADS
    o3 = o2 + c_dsa
    nr = -(-(c_shift + VRES_RANK) // LANES) * LANES
    nd = -(-c_dsa // LANES) * LANES
    tb_in = _row_block(Tp, 576)
    tb_row = _row_block(Tp, 576)
    tb_moe = _row_block(B * Tp, 1152, LANES)
    row1 = lambda v: v.reshape(1, -1)

    h = jnp.concatenate([jnp.broadcast_to(meta_tokens[None].astype(x.dtype), (B, N_META, D)), x,
                         jnp.zeros((B, Tp - T, D), x.dtype)], axis=1)
    tables = _rope_tables(Tp)
    v_first = None
    for l in range(depth):
        w_l = w_in[l]
        w_r = w_l[:, o1:o2]
        mu = mu_shift[l]
        if l > 0:
            w_r = jnp.concatenate([w_r, vres_w_down[l - 1]], axis=1)
            mu = jnp.concatenate([mu, vres_mu[l - 1]])
        gates, ya, pb, pc = _in_proj(
            h, row1(norm_mix[l]), w_l[:, :o0].astype(BF16), w_l[:, o0:o1].astype(BF16),
            _pad_cols(w_r, nr).astype(BF16), _pad_cols(w_l[:, o2:o3], nd).astype(BF16),
            _pad_cols(row1(mu), nr), pool_w[l].astype(BF16), row1(pool_scale[l]), tb_in)

        prep_params = [row1(rwkv_w0[l]), rwkv_w2[l], row1(rwkv_a0[l]), rwkv_a2[l], rwkv_g2[l],
                       row1(rwkv_k_k[l]), row1(rwkv_k_a[l])]
        if l > 0:
            prep_params += [row1(vres_v0[l - 1]), vres_v2[l - 1]]
        r_, lw_, k_, v_, a_, b_, g_ = _rwkv_prep(pb, v_first if l > 0 else None, prep_params, c_b, tb_in)
        if l == 0:
            v_first = v_
        yb = _rwkv_scan(r_, lw_, k_, v_, a_, b_, g_, rwkv_ln_w[l].reshape(npair, 1, GRP),
                        rwkv_ln_b[l].reshape(npair, 1, GRP), rwkv_r_k[l].reshape(npair, 1, GRP))

        w_uq = mla_w_uq[l].reshape(Q_RANK, DSA_HEADS, NOPE + ROPE)
        idx_slab = lambda v: jnp.pad(v, (ROPE, LANES - ROPE - IDX_DIM)).reshape(1, LANES)
        dsa_params = [row1(mla_q_norm[l]), row1(mla_kv_norm[l]),
                      w_uq[:, :, :NOPE].reshape(Q_RANK, DSA_HEADS * NOPE).astype(BF16),
                      w_uq[:, :, NOPE:].reshape(Q_RANK, DSA_HEADS * ROPE).astype(BF16),
                      mla_w_uk[l].astype(BF16), idx_w_q[l], idx_slab(idx_k_norm_w[l]), idx_slab(idx_k_norm_b[l])]
        qcat, kcat, qidx, kidx, wh = _dsa_prep(pc, tables, dsa_params, tb_in)
        key_pad = ((0, 0), (0, -(-Tp // KB) * KB - Tp), (0, 0))
        yc = _dsa(qcat, jnp.pad(kcat, key_pad), qidx, jnp.pad(kidx, key_pad), jnp.swapaxes(wh, 1, 2),
                  mla_w_uv[l].astype(BF16), topk)

        h = _merge(h, gates, ya, yb, yc, w_proj_a[l].astype(BF16), w_proj_b[l].astype(BF16),
                   w_proj_c[l].astype(BF16), w_out[l].astype(BF16), tb_row)

        w13 = jnp.concatenate([expert_w1[l], expert_w3[l]], axis=-1).astype(BF16)
        gpad = ROUTER_EXP0 - N_GROUPS
        w_router = _pad_cols(jnp.concatenate([jnp.pad(router_w_group[l], ((0, 0), (0, gpad))),
                                              router_w_expert[l]], axis=1), LANES)
        b_router = _pad_cols(row1(jnp.concatenate([jnp.pad(router_b_group[l], (0, gpad)), router_b_expert[l]])),
                             LANES)
        h = _moe(h.reshape(B * Tp, D), row1(norm_ffn[l]), w_router, b_router, w13, expert_w2[l].astype(BF16),
                 row1(norm_final), l == depth - 1, tb_moe).reshape(B, Tp, D)
    return h[:, N_META:T]
```

```python
import functools
import math

import jax
import jax.numpy as jnp
from jax import lax
from jax.experimental import pallas as pl
from jax.experimental.pallas import tpu as pltpu

F32 = jnp.float32
BF16 = jnp.bfloat16
I32 = jnp.int32

N_META = 16
EPS = 1e-6
ROPE_THETA = 10000.0
POOL_WINDOWS = (2, 4, 8, 16)
POOL_HALO = 16
HEAD_DIM = 64
GN_EPS = 64e-5
DECAY_RANK = 64
A_RANK = 64
GATE_RANK = 128
VRES_RANK = 32
DSA_HEADS = 16
NOPE = 64
ROPE = 32
V_DIM = 64
Q_RANK = 256
KV_RANK = 128
IDX_HEADS = 8
IDX_DIM = 64
MAX_TOPK = 256
N_GROUPS = 4
EPG = 4
N_EXPERTS = N_GROUPS * EPG
ROUTER_EXP0 = 8

LANES = 128
VMEM_LIMIT = 56 * 1024 * 1024

CHUNK = 64
GRP_HEADS = 4
GRP = GRP_HEADS * HEAD_DIM
SCAN_CHUNKS = 8
INV = 2 * HEAD_DIM
QB = 128
KB = 512
ATT_GROUPS = 4
QCAT = KV_RANK + ROPE
KCAT = 2 * LANES
IDX3 = 3 * IDX_DIM
INT_MIN = -2147483648
SHIFT_SLACK = 100.0

NN = (((1,), (0,)), ((), ()))
NT = (((1,), (1,)), ((), ()))
TN = (((0,), (0,)), ((), ()))


def _dg(a, b, dims=NN):
    return lax.dot_general(a, b, dims, preferred_element_type=F32)


def _mm(a, b, dims=NN):
    return _dg(a.astype(BF16), b.astype(BF16), dims)


def _split(x):
    hi = x.astype(BF16)
    lo = (x - hi.astype(F32)).astype(BF16)
    return hi, lo


def _split3(x):
    a1 = x.astype(BF16)
    r1 = x - a1.astype(F32)
    a2 = r1.astype(BF16)
    a3 = (r1 - a2.astype(F32)).astype(BF16)
    return a1, a2, a3


def _mm3(a, b, dims=NN):
    ah, al = _split(a)
    bh, bl = _split(b)
    return _dg(ah, bh, dims) + (_dg(al, bh, dims) + _dg(ah, bl, dims))


def _mm_rhs_exact(a, m, dims=NN):
    ah, al = _split(a)
    return _dg(ah, m, dims) + _dg(al, m, dims)


def _cparams(sem):
    return pltpu.CompilerParams(dimension_semantics=sem, vmem_limit_bytes=VMEM_LIMIT)


def _row_block(n, target, mult=16):
    best = None
    for d in range(mult, min(n, target) + 1, mult):
        if n % d == 0:
            best = d
    assert best is not None, n
    return best


def _const(shape):
    nd = len(shape)
    return pl.BlockSpec(shape, lambda *_: (0,) * nd, pipeline_mode=pl.Buffered(1))


def _sigmoid(x):
    return 1.0 / (1.0 + jnp.exp(-x))


def _softplus(x):
    return jnp.maximum(x, 0.0) + jnp.log(1.0 + jnp.exp(-jnp.abs(x)))


def _rms(x, g):
    return x * lax.rsqrt(jnp.mean(x * x, axis=-1, keepdims=True) + EPS) * g


def _mm3k(x, w3_ref):
    xh, xl = _split(x)
    return _dg(jnp.concatenate([xh, xl, xh], axis=1), w3_ref[...])


def _rwkv_prep(pb, vf_ref, prm, outs, c):
    use_vres = vf_ref is not None
    if use_vres:
        w0_ref, w2_ref, a0_ref, a2_ref, g2_ref, kk_ref, ka_ref, v0_ref, v2_ref = prm
    else:
        w0_ref, w2_ref, a0_ref, a2_ref, g2_ref, kk_ref, ka_ref = prm
    r_o, lw_o, k_o, v_o, a_o, b_o, g_o = outs
    o = 3 * c
    wd = pb[:, o:o + DECAY_RANK]
    ad = pb[:, o + DECAY_RANK:o + DECAY_RANK + A_RANK]
    gd = pb[:, o + DECAY_RANK + A_RANK:o + DECAY_RANK + A_RANK + GATE_RANK]
    w_log = -_softplus(-(w0_ref[...] + _mm3k(jnp.tanh(wd), w2_ref))) - 0.5
    lw = -jnp.exp(w_log)
    a = _sigmoid(a0_ref[...] + _mm3k(ad, a2_ref))
    g = _mm3k(_sigmoid(gd), g2_ref)
    if use_vres:
        vd = pb[:, o + DECAY_RANK + A_RANK + GATE_RANK:o + DECAY_RANK + A_RANK + GATE_RANK + VRES_RANK]
        vmix = _sigmoid(v0_ref[...] + _mm3k(vd, v2_ref))
    ri = lax.broadcasted_iota(I32, (GRP, GRP), 0) // HEAD_DIM
    ci = lax.broadcasted_iota(I32, (GRP, GRP), 1) // HEAD_DIM
    ones_blk = (ri == ci).astype(BF16)
    for j in range(c // GRP):
        sl = slice(j * GRP, (j + 1) * GRP)
        r = pb[:, j * GRP:(j + 1) * GRP]
        k = pb[:, c + j * GRP:c + (j + 1) * GRP]
        v = pb[:, 2 * c + j * GRP:2 * c + (j + 1) * GRP]
        if use_vres:
            v = v + (vf_ref[0, j] - v) * vmix[:, sl]
        aj = a[:, sl]
        kk = k * kk_ref[:, sl]
        nrm = jnp.sqrt(_mm_rhs_exact(kk * kk, ones_blk))
        kk = kk / jnp.maximum(nrm, 1e-12)
        r_o[0, j] = r
        lw_o[0, j] = lw[:, sl]
        k_o[0, j] = k * (1.0 + (aj - 1.0) * ka_ref[:, sl])
        v_o[0, j] = v
        a_o[0, j] = -kk
        b_o[0, j] = kk * aj
        g_o[0, j] = g[:, sl]


def _in_kernel(*refs, tb, c, use_vres):
    h_ref, g_ref, wg_ref, wp_ref, wr_ref, wd_ref, mu_ref, poolw_ref, pscale_ref = refs[:9]
    n_prm = 9 if use_vres else 7
    vf_ref = refs[9] if use_vres else None
    p0 = 10 if use_vres else 9
    prm = refs[p0:p0 + n_prm]
    gates_ref, ya_ref, pc_ref = refs[p0 + n_prm:p0 + n_prm + 3]
    prep_outs = refs[p0 + n_prm + 3:p0 + n_prm + 10]
    carry_pool, carry_r = refs[p0 + n_prm + 10:]
    t = pl.program_id(1)

    @pl.when(t == 0)
    def _():
        carry_pool[...] = jnp.zeros_like(carry_pool)
        carry_r[...] = jnp.zeros_like(carry_r)

    u = _rms(h_ref[0], g_ref[...]).astype(BF16)

    gates_ref[0] = _sigmoid(_dg(u, wg_ref[...]))
    pc_ref[0] = _dg(u, wd_ref[...])

    p_r = _dg(u, wr_ref[...])
    row = lax.broadcasted_iota(I32, (tb, 1), 0)
    prev = jnp.where(row == 0, carry_r[7:8, :], pltpu.roll(p_r, 1, axis=0))
    carry_r[...] = p_r[tb - 8:, :]
    _rwkv_prep(p_r + (prev - p_r) * mu_ref[...], vf_ref, prm, prep_outs, c)

    p_pool = _dg(u, wp_ref[...])
    ext = jnp.concatenate([carry_pool[...], p_pool], axis=0)
    carry_pool[...] = p_pool[tb - POOL_HALO:, :]
    tpos = (t * tb + row).astype(F32)
    gdim = p_pool.shape[1] // len(POOL_WINDOWS)
    ys = []
    for g, w in enumerate(POOL_WINDOWS):
        s, step = ext[:, g * gdim:(g + 1) * gdim], 1
        while step < w:
            s = s + pltpu.roll(s, step, axis=0)
            step *= 2
        cnt = jnp.minimum(tpos + 1.0, float(w))
        pooled = s[POOL_HALO:, :] / cnt - p_pool[:, g * gdim:(g + 1) * gdim]
        ys.append(_mm(pooled, poolw_ref[g]))
    ya_ref[0] = (jnp.concatenate(ys, axis=-1) * pscale_ref[...]).astype(BF16)


def _in_proj(h, g, wg, wp, wr, wd, mu, poolw, pscale, v_first, prep_params, c, tb):
    B, Tp, D = h.shape
    use_vres = v_first is not None
    ngrp = c // GRP
    blk = lambda n: pl.BlockSpec((1, tb, n), lambda b, t: (b, t, 0))
    grp_blk = pl.BlockSpec((1, ngrp, tb, GRP), lambda b, t: (b, 0, t, 0))
    consts = [g, wg, wp, wr, wd, mu, poolw, pscale]
    ins = [h] + consts + ([v_first] if use_vres else []) + list(prep_params)
    in_specs = ([blk(D)] + [_const(a.shape) for a in consts] + ([grp_blk] if use_vres else [])
                + [_const(p.shape) for p in prep_params])
    return pl.pallas_call(
        functools.partial(_in_kernel, tb=tb, c=c, use_vres=use_vres),
        grid=(B, Tp // tb),
        in_specs=in_specs,
        out_specs=[blk(wg.shape[1]), blk(wp.shape[1]), blk(wd.shape[1])] + [grp_blk] * 7,
        out_shape=[jax.ShapeDtypeStruct((B, Tp, wg.shape[1]), F32),
                   jax.ShapeDtypeStruct((B, Tp, wp.shape[1]), BF16),
                   jax.ShapeDtypeStruct((B, Tp, wd.shape[1]), F32)]
                  + [jax.ShapeDtypeStruct((B, ngrp, Tp, GRP), F32)] * 7,
        scratch_shapes=[pltpu.VMEM((POOL_HALO, wp.shape[1]), F32), pltpu.VMEM((8, wr.shape[1]), F32)],
        compiler_params=_cparams(("arbitrary", "arbitrary")),
        name="in_proj",
    )(*ins)


def _scan_kernel(r_ref, lw_ref, k_ref, v_ref, a_ref, b_ref, g_ref, lnw_ref, lnb_ref, rk_ref, y_ref,
                 *, n_chunks, unroll):
    C = CHUNK

    def rows(c_):
        return pl.ds(c_ * C, C) if isinstance(c_, int) else pl.ds(pl.multiple_of(c_ * C, C), C)

    ri = lax.broadcasted_iota(I32, (GRP, GRP), 0)
    ci = lax.broadcasted_iota(I32, (GRP, GRP), 1)
    same = (ri // C) == (ci // C)
    strict = same & ((ri % C) > (ci % C))
    incl = same & ((ri % C) >= (ci % C))
    eye = (ri == ci).astype(F32)
    ones2 = jnp.concatenate([same, same], axis=0).astype(BF16)
    tri = (lax.broadcasted_iota(I32, (C, C), 0) >= lax.broadcasted_iota(I32, (C, C), 1)).astype(BF16)
    tri3 = jnp.concatenate([tri, tri, tri], axis=1)
    lane_head = lax.broadcasted_iota(I32, (C, GRP), 1) // HEAD_DIM
    lnw, lnb, rk = lnw_ref[0], lnb_ref[0], rk_ref[0]

    def blockdiag(x):
        return jnp.concatenate([jnp.where(lane_head == hd, x, 0.0) for hd in range(GRP_HEADS)], axis=0)

    def each(f, *lists):
        return [f(*xs) for xs in zip(*lists)]

    def chunk_terms(chunk_ids):
        sls = [rows(c_) for c_ in chunk_ids]
        lw = [lw_ref[0, 0, sl, :] for sl in sls]
        cs = each(lambda x: _dg(tri3, jnp.concatenate(_split3(x), axis=0)), lw)
        p_in = each(jnp.exp, cs)
        p_inv = each(lambda x: jnp.exp(-x), cs)
        a_d = each(lambda sl, c, l: blockdiag(a_ref[0, 0, sl, :] * jnp.exp(c - l)), sls, cs, lw)
        r_d = each(lambda sl, p: blockdiag(r_ref[0, 0, sl, :] * p), sls, p_in)
        b_d = each(lambda sl, p: blockdiag(b_ref[0, 0, sl, :] * p).astype(BF16), sls, p_inv)
        k_d = each(lambda sl, p: blockdiag(k_ref[0, 0, sl, :] * p).astype(BF16), sls, p_inv)
        v_d = each(lambda sl: blockdiag(v_ref[0, 0, sl, :]).astype(BF16), sls)
        m = each(lambda a, r, b, k: _mm(jnp.concatenate([a, r], axis=0), jnp.concatenate([b, k], axis=0), NT),
                 a_d, r_d, b_d, k_d)
        l_ab = each(lambda x: jnp.where(strict, x[:GRP, :GRP], 0.0), m)
        lk = each(lambda x: jnp.concatenate([jnp.where(strict, x[:GRP, GRP:], 0.0),
                                             jnp.where(incl, x[GRP:, GRP:], 0.0)], axis=0), m)
        m_rb = each(lambda x: jnp.where(incl, x[GRP:, :GRP], 0.0), m)
        lv = each(_mm, lk, v_d)
        parts = [slice(q * INV, (q + 1) * INV) for q in range(GRP // INV)]
        eye_p = eye[:INV, :INV]
        lp = [x[q, q] for x in l_ab for q in parts]
        t = each(lambda x: eye_p + x, lp)
        lp = each(lambda x: _mm(x, x), lp)
        step = 2
        while step < C:
            if 2 * step < C:
                x = each(lambda p, q: _mm(p, jnp.concatenate([q, p], axis=1)), lp, t)
                t = each(lambda q, y: q + y[:, :INV], t, x)
                lp = each(lambda y: y[:, INV:], x)
            else:
                t = each(lambda q, p: q + _mm(p, q), t, lp)
            step *= 2
        rhs = each(lambda a, y: jnp.concatenate([a, y[:GRP]], axis=1), a_d, lv)
        npart = len(parts)
        w = [jnp.concatenate([_mm(t[i * npart + qi], rhs[i][q]) for qi, q in enumerate(parts)], axis=0)
             for i in range(len(rhs))]
        rbw = each(_mm, m_rb, w)
        r2 = each(lambda r, y: r + y[:, :GRP], r_d, rbw)
        o0 = each(lambda y, z: y[:, GRP:] + z[GRP:], rbw, lv)
        tw = each(lambda x, b: _mm(x, b, TN), w, b_d)
        vk = each(lambda v, k: _mm(v, k, TN), v_d, k_d)
        gmat = each(lambda x, p: (eye + x[:GRP]) * p[C - 1:C, :], tw, p_in)
        hmat = each(lambda x, y, p: (x[GRP:] + y) * p[C - 1:C, :], tw, vk, p_in)
        return list(zip(r2, o0, gmat, hmat))

    def cat_split(x):
        hi, lo = _split(x)
        return jnp.concatenate([hi, lo], axis=1)

    def advance(chunk_ids, s):
        sls = [rows(c_) for c_ in chunk_ids]
        ods = []
        for r2, o0, gmat, hmat in chunk_terms(chunk_ids):
            ods.append(_mm(r2, s, NT) + o0)
            sh, slo = _split(s)
            gh, glo = _split(gmat)
            s = _dg(jnp.concatenate([sh, slo, sh], axis=1), jnp.concatenate([gh, gh, glo], axis=0)) + hmat
        o = each(lambda od: sum(od[hd * C:(hd + 1) * C] for hd in range(1, GRP_HEADS)) + od[:C], ods)
        sums = each(lambda sl, o_: _dg(cat_split(jnp.concatenate(
            [o_, r_ref[0, 0, sl, :] * k_ref[0, 0, sl, :] * rk], axis=0)), ones2), sls, o)
        d = each(lambda o_, z: o_ - z[:C] * (1.0 / HEAD_DIM), o, sums)
        var = each(lambda x: _dg(cat_split(x * x), ones2) * (1.0 / HEAD_DIM), d)
        for sl, d_, var_, z in zip(sls, d, var, sums):
            on = d_ * lax.rsqrt(var_ + GN_EPS) * lnw + lnb
            y_ref[0, 0, sl, :] = ((on + z[C:] * v_ref[0, 0, sl, :]) * g_ref[0, 0, sl, :]).astype(BF16)
        return s

    n_main = n_chunks // unroll
    s = lax.fori_loop(0, n_main, lambda it, s_: advance([it * unroll + u for u in range(unroll)], s_),
                      jnp.zeros((GRP, GRP), F32))
    if n_chunks % unroll:
        advance(list(range(n_main * unroll, n_chunks)), s)


def _rwkv_scan(r, lw, k, v, a, b, g, lnw, lnb, rk):
    B, ngrp, Tp, _ = r.shape
    n_chunks = Tp // CHUNK
    unroll = min(SCAN_CHUNKS, n_chunks)
    seq = pl.BlockSpec((1, 1, Tp, GRP), lambda bi, j: (bi, j, 0, 0))
    par = pl.BlockSpec((1, 1, GRP), lambda bi, j: (j, 0, 0))
    return pl.pallas_call(
        functools.partial(_scan_kernel, n_chunks=n_chunks, unroll=unroll),
        grid=(B, ngrp),
        in_specs=[seq] * 7 + [par] * 3,
        out_specs=seq,
        out_shape=jax.ShapeDtypeStruct((B, ngrp, Tp, GRP), BF16),
        compiler_params=_cparams(("arbitrary", "arbitrary")),
        name="rwkv_scan",
    )(r, lw, k, v, a, b, g, lnw, lnb, rk)


def _swap16(x):
    n = x.shape[-1]
    lane = lax.broadcasted_iota(I32, x.shape, x.ndim - 1)
    return jnp.where((lane % 32) < 16, pltpu.roll(x, n - 16, axis=x.ndim - 1), pltpu.roll(x, 16, axis=x.ndim - 1))


def _dsa_prep_kernel(pc_ref, cq_t, sq_t, ci_t, si_t, ck_t, sk_t, gq_ref, gkv_ref, wn_ref, wr_ref, wuk_ref,
                     wqi_ref, lnw_ref, lnb_ref, qcat_o, kcat_o, qidx_o, kidx_o, wh_o, *, scale):
    cq = _rms(pc_ref[0, :, :Q_RANK], gq_ref[...])
    ckv = _rms(pc_ref[0, :, Q_RANK:Q_RANK + KV_RANK], gkv_ref[...])
    slab = pc_ref[0, :, Q_RANK + KV_RANK:Q_RANK + KV_RANK + LANES]
    lane = lax.broadcasted_iota(I32, slab.shape, 1)
    in_idx = (lane >= ROPE) & (lane < ROPE + IDX_DIM)
    mean = jnp.sum(jnp.where(in_idx, slab, 0.0), axis=-1, keepdims=True) * (1.0 / IDX_DIM)
    dev = jnp.where(in_idx, slab - mean, 0.0)
    var = jnp.sum(dev * dev, axis=-1, keepdims=True) * (1.0 / IDX_DIM)
    kn = dev * lax.rsqrt(var + EPS) * lnw_ref[...] + lnb_ref[...]
    kslab = jnp.where(lane < ROPE, slab, kn)
    kslab = kslab * ck_t[...] + _swap16(kslab) * sk_t[...]
    tail = jnp.where(lane < ROPE, kslab, 0.0)
    kcat_o[0] = jnp.concatenate([ckv, tail], axis=-1).astype(BF16)
    khi, klo = _split(kslab[:, ROPE:ROPE + IDX_DIM])
    kidx_o[0] = jnp.concatenate([khi, khi, klo], axis=-1)
    wh_o[0] = slab[:, ROPE + IDX_DIM:ROPE + IDX_DIM + IDX_HEADS] * (IDX_HEADS ** -0.5 * IDX_DIM ** -0.5)

    q_nope = _mm(cq, wn_ref[...])
    q_rope = _mm(cq, wr_ref[...])
    q_rope = q_rope * cq_t[...] + _swap16(q_rope) * sq_t[...]
    q_idx = _mm3(cq, wqi_ref[...])
    q_idx = q_idx * ci_t[...] + _swap16(q_idx) * si_t[...]
    for hd in range(DSA_HEADS):
        q_lat = _mm(q_nope[:, hd * NOPE:(hd + 1) * NOPE], wuk_ref[hd], NT)
        qc = jnp.concatenate([q_lat, q_rope[:, hd * ROPE:(hd + 1) * ROPE]], axis=-1) * scale
        qcat_o[0, hd] = qc.astype(BF16)
    for hd in range(IDX_HEADS):
        qhi, qlo = _split(q_idx[:, hd * IDX_DIM:(hd + 1) * IDX_DIM])
        qidx_o[0, hd] = jnp.concatenate([qhi, qlo, qhi], axis=-1)


def _dsa_prep(pc, tables, params, tb):
    B, Tp, ND = pc.shape
    row = lambda c: pl.BlockSpec((1, tb, c), lambda b, t: (b, t, 0))
    tab = lambda a: pl.BlockSpec((tb, a.shape[1]), lambda b, t: (t, 0))
    head = lambda n, c: pl.BlockSpec((1, n, tb, c), lambda b, t: (b, 0, t, 0))
    scale = float((NOPE + ROPE) ** -0.5 * math.log2(math.e))
    return pl.pallas_call(
        functools.partial(_dsa_prep_kernel, scale=scale),
        grid=(B, Tp // tb),
        in_specs=[row(ND)] + [tab(a) for a in tables] + [_const(p.shape) for p in params],
        out_specs=[head(DSA_HEADS, QCAT), row(KCAT), head(IDX_HEADS, IDX3), row(IDX3), row(IDX_HEADS)],
        out_shape=[jax.ShapeDtypeStruct((B, DSA_HEADS, Tp, QCAT), BF16),
                   jax.ShapeDtypeStruct((B, Tp, KCAT), BF16),
                   jax.ShapeDtypeStruct((B, IDX_HEADS, Tp, IDX3), BF16),
                   jax.ShapeDtypeStruct((B, Tp, IDX3), BF16),
                   jax.ShapeDtypeStruct((B, Tp, IDX_HEADS), F32)],
        compiler_params=_cparams(("arbitrary", "arbitrary")),
        name="dsa_prep",
    )(pc, *tables, *params)


def _dsa_kernel(qcat_ref, qidx_ref, wh_ref, kcat_ref, kidx_ref, wuv_ref, y_ref,
                keys_ref, mask_ref, mx_ref, top_ref, acc_ref, *, topk, pos_bits):
    i = pl.program_id(1)
    nj = (i * QB + QB + KB - 1) // KB
    qb_per_chunk = KB // QB
    short_tail = (i % qb_per_chunk) < qb_per_chunk // 2
    def qpos(w):
        return i * QB + lax.broadcasted_iota(I32, (w, QB), 1)

    def kpos(j, w):
        return j * KB + lax.broadcasted_iota(I32, (w, QB), 0)

    q_idx = qidx_ref[0].reshape(IDX_HEADS * QB, IDX3)
    w_head = wh_ref[0]

    def tail_chunk(body, *args):
        return lax.cond(short_tail, lambda *a: body(nj - 1, KB // 2, *a), lambda *a: body(nj - 1, KB, *a), *args)

    def for_chunks(body, n_full=None, tail=True):
        def full(j, carry):
            body(j, KB)
            return carry
        lax.fori_loop(0, nj - 1 if n_full is None else n_full, full, 0)
        if tail is True:
            tail_chunk(body)
        else:
            pl.when(tail)(lambda: tail_chunk(body))

    def score_chunk(j, w):
        kc = kidx_ref[0, pl.ds(pl.multiple_of(j * KB, KB), w), :]
        s = jnp.maximum(_dg(kc, q_idx, NT), 0.0)
        tot = s[:, :QB] * w_head[0:1, :]
        for hd in range(1, IDX_HEADS):
            tot = tot + s[:, hd * QB:(hd + 1) * QB] * w_head[hd:hd + 1, :]
        tot = jnp.where(tot == 0.0, 0.0, tot)
        bits = pltpu.bitcast(tot, I32)
        key = jnp.where(bits < 0, bits ^ 0x7FFFFFFF, bits)
        keys_ref.at[j][:w, :] = jnp.where(kpos(j, w) <= qpos(w), key, INT_MIN)

    for_chunks(score_chunk)

    def count(pred):
        def step(j, w, acc):
            hit = pred(keys_ref.at[j][:w, :], j, w).astype(I32)
            return acc + jnp.sum(hit.reshape(w // 8, 8, QB), axis=0)
        acc = lax.fori_loop(0, nj - 1, lambda j, a: step(j, KB, a), jnp.zeros((8, QB), I32))
        return jnp.sum(tail_chunk(step, acc), axis=0, keepdims=True)

    c0 = count(lambda kk, j, w: kk >= 0)
    v0 = jnp.where(c0 >= topk, 0, INT_MIN).astype(I32)

    def vbit(t, carry):
        v, n_ge = carry
        cand = v + lax.shift_left(jnp.int32(1), 30 - t)
        c = count(lambda kk, j, w: kk >= cand)
        return jnp.where(c >= topk, cand, v), jnp.where(c >= topk, c, n_ge)

    vth, n_ge = lax.fori_loop(0, 31, vbit, (v0, c0))
    excess_ties = jnp.max(((n_ge > topk) & (vth > INT_MIN)).astype(I32)) > 0

    def tie_break():
        need = topk - count(lambda kk, j, w: kk > vth)

        def jbit(t, jv):
            cand = jv + lax.shift_left(jnp.int32(1), pos_bits - 1 - t)
            c = count(lambda kk, j, w: (kk == vth) & (kpos(j, w) < cand))
            return jnp.where(c < need, cand, jv)
        return lax.fori_loop(0, pos_bits, jbit, jnp.zeros((1, QB), I32))

    jth = lax.cond(excess_ties, tie_break, lambda: jnp.full((1, QB), (1 << pos_bits) - 1, I32))

    def mask_chunk(j, w):
        key = keys_ref.at[j][:w, :]
        kp = kpos(j, w)
        sel = (key >= vth) & ((key > vth) | (kp <= jth)) & (kp <= qpos(w))
        sel = sel.astype(F32)
        for c in range(w // LANES):
            mask_ref.at[j][:, c * LANES:(c + 1) * LANES] = jnp.transpose(sel[c * LANES:(c + 1) * LANES, :])

    for_chunks(mask_chunk)

    q_all = qcat_ref[0].reshape(DSA_HEADS * QB, QCAT)
    q_all = jnp.concatenate([q_all, jnp.zeros((DSA_HEADS * QB, KCAT - QCAT), BF16)], axis=1)
    hpg = DSA_HEADS // ATT_GROUPS
    rows = hpg * QB

    def masked_scores(g, kc, msk):
        s = _dg(q_all[g * rows:(g + 1) * rows], kc, NT).reshape(hpg, QB, kc.shape[0])
        return jnp.where(msk[None], s, -jnp.inf)

    def lane_max(ref, g, s):
        m = ref[g * rows:(g + 1) * rows, :].reshape(hpg, QB, LANES)
        for c in range(s.shape[-1] // LANES):
            m = jnp.maximum(m, s[:, :, c * LANES:(c + 1) * LANES])
        ref[g * rows:(g + 1) * rows, :] = m.reshape(rows, LANES)

    def row_max_pass(first_only):
        mx_ref[...] = jnp.full_like(mx_ref, -jnp.inf)

        def row_max(j, w):
            kc = kcat_ref[0, pl.ds(pl.multiple_of(j * KB, KB), w), :]
            msk = mask_ref.at[j][:, :w] > 0.5
            for g in range(ATT_GROUPS):
                lane_max(mx_ref, g, masked_scores(g, kc, msk))

        if first_only:
            for_chunks(row_max, n_full=jnp.minimum(1, nj - 1), tail=nj == 1)
        else:
            for_chunks(row_max)
        mx_ref[...] = jnp.broadcast_to(jnp.max(mx_ref[...], axis=-1, keepdims=True), mx_ref.shape)

    def attend_pass(track):
        acc_ref[...] = jnp.zeros_like(acc_ref)
        if track:
            top_ref[...] = jnp.full_like(top_ref, -jnp.inf)

        def attend(j, w):
            kc = kcat_ref[0, pl.ds(pl.multiple_of(j * KB, KB), w), :]
            val = jnp.concatenate([kc[:, :KV_RANK], jnp.ones((w, KCAT - KV_RANK), BF16)], axis=1)
            msk = mask_ref.at[j][:, :w] > 0.5
            for g in range(ATT_GROUPS):
                s = masked_scores(g, kc, msk)
                if track:
                    lane_max(top_ref, g, s)
                m = mx_ref[g * rows:(g + 1) * rows, :].reshape(hpg, QB, LANES)
                p = jnp.concatenate([jnp.exp2(s[:, :, c * LANES:(c + 1) * LANES] - m)
                                     for c in range(w // LANES)], axis=-1)
                acc_ref[g * rows:(g + 1) * rows, :] += _dg(p.astype(BF16).reshape(rows, w), val)

        for_chunks(attend)

    row_max_pass(first_only=True)
    attend_pass(track=True)
    covered = top_ref[...] - mx_ref[...] <= SHIFT_SLACK
    uncovered = jnp.max(jnp.where(covered, 0, 1)) > 0

    @pl.when(uncovered)
    def _():
        row_max_pass(first_only=False)
        attend_pass(track=False)

    o_lat = acc_ref[:, :KV_RANK] / acc_ref[:, KV_RANK:]
    y_ref[0] = jnp.concatenate([_mm(o_lat[hd * QB:(hd + 1) * QB], wuv_ref[hd]) for hd in range(DSA_HEADS)],
                               axis=-1).astype(BF16)


def _dsa(qcat, kcat, qidx, kidx, wh, wuv, topk):
    B, _, Tp, _ = qcat.shape
    Tk = kcat.shape[1]
    pos_bits = max(1, (Tk - 1).bit_length())
    qh = lambda n, c: pl.BlockSpec((1, n, QB, c), lambda b, i: (b, 0, i, 0))
    whole = lambda c: pl.BlockSpec((1, Tk, c), lambda b, i: (b, 0, 0))
    return pl.pallas_call(
        functools.partial(_dsa_kernel, topk=topk, pos_bits=pos_bits),
        grid=(B, Tp // QB),
        in_specs=[qh(DSA_HEADS, QCAT), qh(IDX_HEADS, IDX3), pl.BlockSpec((1, IDX_HEADS, QB), lambda b, i: (b, 0, i)),
                  whole(KCAT), whole(IDX3), _const(wuv.shape)],
        out_specs=pl.BlockSpec((1, QB, DSA_HEADS * V_DIM), lambda b, i: (b, i, 0)),
        out_shape=jax.ShapeDtypeStruct((B, Tp, DSA_HEADS * V_DIM), BF16),
        scratch_shapes=[pltpu.VMEM((Tk // KB, KB, QB), I32),
                        pltpu.VMEM((Tk // KB, QB, KB), F32),
                        pltpu.VMEM((DSA_HEADS * QB, LANES), F32),
                        pltpu.VMEM((DSA_HEADS * QB, LANES), F32),
                        pltpu.VMEM((DSA_HEADS * QB, KCAT), F32)],
        compiler_params=_cparams(("arbitrary", "arbitrary")),
        name="dsa_attn",
    )(qcat, qidx, wh, kcat, kidx, wuv)


def _merge_kernel(h_ref, gates_ref, ya_ref, yb_ref, yc_ref, pa_ref, pb_ref, pc_ref, wo_ref, o_ref, *, d):
    za = _dg(ya_ref[0], pa_ref[...])
    zb = _dg(yb_ref[0, 0], pb_ref[0:GRP, :])
    for j in range(1, yb_ref.shape[1]):
        zb = zb + _dg(yb_ref[0, j], pb_ref[j * GRP:(j + 1) * GRP, :])
    zc = _dg(yc_ref[0], pc_ref[...])
    merged = gates_ref[0, :, :d] * za + gates_ref[0, :, d:2 * d] * zb + gates_ref[0, :, 2 * d:] * zc
    o_ref[0] = h_ref[0] + _mm(merged, wo_ref[...])


def _merge(h, gates, ya, yb, yc, pa, pb, pc, wo, tb):
    B, Tp, D = h.shape
    row = lambda c: pl.BlockSpec((1, tb, c), lambda b, t: (b, t, 0))
    return pl.pallas_call(
        functools.partial(_merge_kernel, d=D),
        grid=(B, Tp // tb),
        in_specs=[row(D), row(gates.shape[2]), row(ya.shape[2]),
                  pl.BlockSpec((1, yb.shape[1], tb, GRP), lambda b, t: (b, 0, t, 0)), row(yc.shape[2])]
                 + [_const(w.shape) for w in (pa, pb, pc, wo)],
        out_specs=row(D),
        out_shape=jax.ShapeDtypeStruct((B, Tp, D), F32),
        compiler_params=_cparams(("arbitrary", "arbitrary")),
        name="merge",
    )(h, gates, ya, yb, yc, pa, pb, pc, wo)


def _first_index_of_max(x, valid, idx):
    big = jnp.int32(1 << 20)
    mx = jnp.max(jnp.where(valid, x, -jnp.inf), axis=0, keepdims=True)
    first = jnp.min(jnp.where(valid & (x == mx), idx, big), axis=0, keepdims=True)
    return mx, first


def _moe_kernel(h_ref, gn_ref, wr_ref, br_ref, w13_ref, w2_ref, gf_ref, o_ref,
                hn_ref, comb_ref, *, hidden, final):
    e = pl.program_id(1)

    @pl.when(e == 0)
    def _():
        h = h_ref[...]
        hn = _rms(h, gn_ref[...])
        hn_ref[...] = hn.astype(BF16)
        lt = jnp.transpose(_mm3(hn, wr_ref[...]) + br_ref[...])
        gidx = lax.broadcasted_iota(I32, (ROUTER_EXP0, lt.shape[1]), 0)
        is_grp = gidx < N_GROUPS
        gl = lt[:ROUTER_EXP0]
        gmax, gsel = _first_index_of_max(gl, is_grp, gidx)
        p_grp = 1.0 / jnp.sum(jnp.where(is_grp, jnp.exp(gl - gmax), 0.0), axis=0, keepdims=True)
        el = lt[ROUTER_EXP0:ROUTER_EXP0 + N_EXPERTS]
        eidx = lax.broadcasted_iota(I32, el.shape, 0)
        in_grp = (eidx // EPG) == gsel
        t1, i1 = _first_index_of_max(el, in_grp, eidx)
        t2, i2 = _first_index_of_max(el, in_grp & (eidx != i1), eidx)
        e2 = jnp.exp(t2 - t1)
        w1 = p_grp / (1.0 + e2)
        w2 = p_grp * e2 / (1.0 + e2)
        comb = jnp.where(eidx == i1, w1, 0.0) + jnp.where(eidx == i2, w2, 0.0)
        comb_ref[...] = jnp.transpose(
            jnp.concatenate([comb, jnp.zeros((LANES - N_EXPERTS, comb.shape[1]), F32)], axis=0))
        o_ref[...] = h

    hn = hn_ref[...]
    lane = lax.broadcasted_iota(I32, comb_ref.shape, 1)
    ce = jnp.sum(jnp.where(lane == e, comb_ref[...], 0.0), axis=-1, keepdims=True)
    x13 = _dg(hn, w13_ref[0])
    x1, x3 = x13[:, :hidden], x13[:, hidden:]
    hid = x1 * _sigmoid(x1) * x3 * ce
    o_ref[...] += _mm(hid, w2_ref[0])

    if final:
        @pl.when(e == pl.num_programs(1) - 1)
        def _():
            o_ref[...] = _rms(o_ref[...], gf_ref[...])


def _moe(h2, gn, wr, br, w13, w2, gf, final, tb):
    n, D = h2.shape
    E, _, H2 = w13.shape
    row = pl.BlockSpec((tb, D), lambda r, e: (r, 0))
    return pl.pallas_call(
        functools.partial(_moe_kernel, hidden=H2 // 2, final=final),
        grid=(n // tb, E),
        in_specs=[row] + [_const(a.shape) for a in (gn, wr, br)]
                 + [pl.BlockSpec((1, D, H2), lambda r, e: (e, 0, 0)),
                    pl.BlockSpec((1, H2 // 2, D), lambda r, e: (e, 0, 0)), _const(gf.shape)],
        out_specs=row,
        out_shape=jax.ShapeDtypeStruct((n, D), F32),
        scratch_shapes=[pltpu.VMEM((tb, D), BF16), pltpu.VMEM((tb, LANES), F32)],
        compiler_params=_cparams(("arbitrary", "arbitrary")),
        name="moe",
    )(h2, gn, wr, br, w13, w2, gf)


def _rope_tables(tp):
    inv = ROPE_THETA ** (-jnp.arange(0, ROPE, 2, dtype=F32) / ROPE)
    ang = jnp.arange(tp, dtype=F32)[:, None] * inv[None, :]
    cos, sin = jnp.cos(ang), jnp.sin(ang)
    c32 = jnp.concatenate([cos, cos], axis=-1)
    s32 = jnp.concatenate([-sin, sin], axis=-1)
    one32, zero32 = jnp.ones_like(c32), jnp.zeros_like(c32)
    cq, sq = jnp.tile(c32, (1, DSA_HEADS)), jnp.tile(s32, (1, DSA_HEADS))
    ci = jnp.tile(jnp.concatenate([c32, one32], axis=-1), (1, IDX_HEADS))
    si = jnp.tile(jnp.concatenate([s32, zero32], axis=-1), (1, IDX_HEADS))
    ck = jnp.concatenate([c32, c32, one32, one32], axis=-1)
    sk = jnp.concatenate([s32, s32, zero32, zero32], axis=-1)
    return cq, sq, ci, si, ck, sk


def _pad_cols(w, n):
    return jnp.pad(w, ((0, 0), (0, n - w.shape[1])))


def _hi_hi_lo(w):
    hi, lo = _split(w)
    return jnp.concatenate([hi, hi, lo], axis=0)


def kernel(x, meta_tokens, norm_mix, w_in, mu_shift, pool_w, pool_scale, rwkv_w0, rwkv_w2, rwkv_a0, rwkv_a2, rwkv_g2, rwkv_k_k, rwkv_k_a, rwkv_r_k, rwkv_ln_w, rwkv_ln_b, vres_w_down, vres_mu, vres_v0, vres_v2, mla_q_norm, mla_kv_norm, mla_w_uq, mla_w_uk, mla_w_uv, idx_w_q, idx_k_norm_w, idx_k_norm_b, w_proj_a, w_proj_b, w_proj_c, w_out, norm_ffn, router_w_group, router_b_group, router_w_expert, router_b_expert, expert_w1, expert_w3, expert_w2, norm_final):
    B, S, D = x.shape
    depth = w_in.shape[0]
    T = S + N_META
    Tp = -(-T // LANES) * LANES
    topk = min(MAX_TOPK, S // 4)
    c_a = pool_scale.shape[1]
    c_b = rwkv_w0.shape[1]
    c_shift = mu_shift.shape[1]
    npair = c_b // GRP
    o0 = 3 * D
    o1 = o0 + c_a
    o2 = o1 + c_shift
    c_dsa = Q_RANK + KV_RANK + ROPE + IDX_DIM + IDX_HEADS
    o3 = o2 + c_dsa
    nr = -(-(c_shift + VRES_RANK) // LANES) * LANES
    nd = -(-c_dsa // LANES) * LANES
    tb_in = _row_block(Tp, 288)
    tb_row = _row_block(Tp, 576)
    tb_moe = _row_block(B * Tp, 1152, LANES)
    row1 = lambda v: v.reshape(1, -1)

    h = jnp.concatenate([jnp.broadcast_to(meta_tokens[None].astype(x.dtype), (B, N_META, D)), x,
                         jnp.zeros((B, Tp - T, D), x.dtype)], axis=1)
    tables = _rope_tables(Tp)
    v_first = None
    for l in range(depth):
        w_l = w_in[l]
        w_r = w_l[:, o1:o2]
        mu = mu_shift[l]
        if l > 0:
            w_r = jnp.concatenate([w_r, vres_w_down[l - 1]], axis=1)
            mu = jnp.concatenate([mu, vres_mu[l - 1]])
        prep_params = [row1(rwkv_w0[l]), _hi_hi_lo(rwkv_w2[l]), row1(rwkv_a0[l]), _hi_hi_lo(rwkv_a2[l]),
                       _hi_hi_lo(rwkv_g2[l]), row1(rwkv_k_k[l]), row1(rwkv_k_a[l])]
        if l > 0:
            prep_params += [row1(vres_v0[l - 1]), _hi_hi_lo(vres_v2[l - 1])]
        gates, ya, pc, r_, lw_, k_, v_, a_, b_, g_ = _in_proj(
            h, row1(norm_mix[l]), w_l[:, :o0].astype(BF16), w_l[:, o0:o1].astype(BF16),
            _pad_cols(w_r, nr).astype(BF16), _pad_cols(w_l[:, o2:o3], nd).astype(BF16),
            _pad_cols(row1(mu), nr), pool_w[l].astype(BF16), row1(pool_scale[l]),
            v_first if l > 0 else None, prep_params, c_b, tb_in)
        if l == 0:
            v_first = v_
        yb = _rwkv_scan(r_, lw_, k_, v_, a_, b_, g_, rwkv_ln_w[l].reshape(npair, 1, GRP),
                        rwkv_ln_b[l].reshape(npair, 1, GRP), rwkv_r_k[l].reshape(npair, 1, GRP))

        w_uq = mla_w_uq[l].reshape(Q_RANK, DSA_HEADS, NOPE + ROPE)
        idx_slab = lambda v: jnp.pad(v, (ROPE, LANES - ROPE - IDX_DIM)).reshape(1, LANES)
        dsa_params = [row1(mla_q_norm[l]), row1(mla_kv_norm[l]),
                      w_uq[:, :, :NOPE].reshape(Q_RANK, DSA_HEADS * NOPE).astype(BF16),
                      w_uq[:, :, NOPE:].reshape(Q_RANK, DSA_HEADS * ROPE).astype(BF16),
                      mla_w_uk[l].astype(BF16), idx_w_q[l], idx_slab(idx_k_norm_w[l]), idx_slab(idx_k_norm_b[l])]
        qcat, kcat, qidx, kidx, wh = _dsa_prep(pc, tables, dsa_params, tb_in)
        key_pad = ((0, 0), (0, -(-Tp // KB) * KB - Tp), (0, 0))
        yc = _dsa(qcat, jnp.pad(kcat, key_pad), qidx, jnp.pad(kidx, key_pad), jnp.swapaxes(wh, 1, 2),
                  mla_w_uv[l].astype(BF16), topk)

        h = _merge(h, gates, ya, yb, yc, w_proj_a[l].astype(BF16), w_proj_b[l].astype(BF16),
                   w_proj_c[l].astype(BF16), w_out[l].astype(BF16), tb_row)

        w13 = jnp.concatenate([expert_w1[l], expert_w3[l]], axis=-1).astype(BF16)
        gpad = ROUTER_EXP0 - N_GROUPS
        w_router = _pad_cols(jnp.concatenate([jnp.pad(router_w_group[l], ((0, 0), (0, gpad))),
                                              router_w_expert[l]], axis=1), LANES)
        b_router = _pad_cols(row1(jnp.concatenate([jnp.pad(router_b_group[l], (0, gpad)), router_b_expert[l]])),
                             LANES)
        h = _moe(h.reshape(B * Tp, D), row1(norm_ffn[l]), w_router, b_router, w13, expert_w2[l].astype(BF16),
                 row1(norm_final), l == depth - 1, tb_moe).reshape(B, Tp, D)
    return h[:, N_META:T]
```
